```python
import functools
import jax, jax.numpy as jnp
from jax import lax
import numpy as np

D_MODEL = 1024
BATCH = 32
SEQ = 256
DEPTH = 4
DEC_BATCH = 2
DEC_SEQ = 2048
PAST_LEN = 256

GRID_W = 64
WIN_H = 8
WIN_W = 16
HEAD_DIM = 64
ATTN_WIDTH = D_MODEL // 2
N_HEADS = ATTN_WIDTH // HEAD_DIM
SGU_WIDTH = D_MODEL // 4
SGU_GROUPS = 4
SGU_GROUP_DIM = SGU_WIDTH // SGU_GROUPS
CHUNK = 128
FNET_WIDTH = D_MODEL // 4
FNET_GROUPS = 4
FNET_GROUP_DIM = FNET_WIDTH // FNET_GROUPS
N_BRANCH = 3
D_FF = 2816
N_MOD = 9
Q_BLOCK = 128
IN_SPLITS = [ATTN_WIDTH, 2 * ATTN_WIDTH, 3 * ATTN_WIDTH, 3 * ATTN_WIDTH + SGU_WIDTH,
             3 * ATTN_WIDTH + 2 * SGU_WIDTH, 3 * ATTN_WIDTH + 2 * SGU_WIDTH + FNET_WIDTH]
IN_COLS = IN_SPLITS[-1] + N_BRANCH * D_MODEL
MOD_SCALE = 0.3
RMS_EPS = 1e-6
NEG_INF = -1e30

kernel_name = 'hybrid_natten_sgu_fnet_macaron_step'


def _rmsnorm(x, g):
    xf = x.astype(jnp.float32)
    y = xf * lax.rsqrt(jnp.mean(xf * xf, axis=-1, keepdims=True) + RMS_EPS)
    return (y * g.astype(jnp.float32)).astype(x.dtype)


def _modulate(x, shift, scale):
    return x * (1 + scale) + shift


def _swiglu(x, w_gate, w_up, w_down):
    return (jax.nn.silu(x @ w_gate) * (x @ w_up)) @ w_down


def _context_attention(q, k, v, rpb_l):
    B, L, H, dh = q.shape
    qb = jnp.swapaxes(q.reshape(B, L // Q_BLOCK, Q_BLOCK, H, dh), 0, 1)

    def block(qi):
        s = jnp.einsum('bqhd,blhd->bhql', qi, k).astype(jnp.float32) * (dh ** -0.5)
        p = jax.nn.softmax(s, axis=-1).astype(v.dtype)
        return jnp.einsum('bhql,blhd->bqhd', p, v)

    o = lax.map(block, qb)
    return jnp.swapaxes(o, 0, 1).reshape(B, L, H, dh)


def _natten_latent(q, k, v, rpb_l, kc, vc):
    B, T, H, dh = q.shape
    rows = T // GRID_W
    kh = min(WIN_H, rows)
    qr = q.reshape(B, rows, GRID_W, H, dh)
    kr = k.reshape(B, rows, GRID_W, H, dh)
    vr = v.reshape(B, rows, GRID_W, H, dh)
    cols = jnp.arange(GRID_W)
    col_start = jnp.clip(cols - WIN_W // 2, 0, GRID_W - WIN_W)
    col_ok = (cols[None, :] >= col_start[:, None]) & (cols[None, :] < col_start[:, None] + WIN_W)
    dcol = jnp.clip(cols[None, :] - cols[:, None], -(WIN_W - 1), WIN_W - 1) + (WIN_W - 1)
    scale = dh ** -0.5

    def row_block(r):
        r0 = jnp.clip(r - kh // 2, 0, rows - kh)
        q_blk = lax.dynamic_index_in_dim(qr, r, axis=1, keepdims=False)
        k_blk = lax.dynamic_slice_in_dim(kr, r0, kh, axis=1)
        v_blk = lax.dynamic_slice_in_dim(vr, r0, kh, axis=1)
        drow = r0 + jnp.arange(kh) - r + (WIN_H - 1)
        bias = rpb_l[:, drow[None, :, None], dcol[:, None, :]]
        s_loc = (jnp.einsum('bqhd,bkwhd->bhqkw', q_blk, k_blk).astype(jnp.float32) * scale
                 + bias[None].astype(jnp.float32))
        s_loc = jnp.where(col_ok[:, None, :], s_loc, NEG_INF).reshape(B, H, GRID_W, kh * GRID_W)
        s_ctx = jnp.einsum('bqhd,blhd->bhql', q_blk, kc).astype(jnp.float32) * scale
        p = jax.nn.softmax(jnp.concatenate([s_loc, s_ctx], axis=-1), axis=-1).astype(v.dtype)
        p_loc = p[..., :kh * GRID_W].reshape(B, H, GRID_W, kh, GRID_W)
        p_ctx = p[..., kh * GRID_W:]
        return (jnp.einsum('bhqkw,bkwhd->bqhd', p_loc, v_blk)
                + jnp.einsum('bhql,blhd->bqhd', p_ctx, vc))

    o = lax.map(row_block, jnp.arange(rows))
    return jnp.moveaxis(o, 0, 1).reshape(B, T, H, dh)


def _chunk_sgu(u, v, g, w_s, b_s):
    B, T, _ = v.shape
    vn = _rmsnorm(v, g).reshape(B, T // CHUNK, CHUNK, SGU_GROUPS, SGU_GROUP_DIM)
    mixed = jnp.einsum('gpq,bnqgc->bnpgc', w_s, vn) + jnp.swapaxes(b_s, 0, 1)[:, :, None]
    return u * mixed.reshape(B, T, SGU_WIDTH)


def _fourier_mix(f):
    B, T, _ = f.shape
    fg = f.astype(jnp.float32).reshape(B, T, FNET_GROUPS, FNET_GROUP_DIM)
    out = jnp.fft.fftn(fg, axes=(1, 3), norm='ortho').real
    return out.reshape(B, T, FNET_WIDTH).astype(f.dtype)


def _layer(h, cond, l, W, attend):
    B, T, _ = h.shape
    m = (jax.nn.silu(cond) @ W['w_mod'][l] + W['b_mod'][l]).reshape(cond.shape[0], 1, N_MOD, D_MODEL)
    z = _modulate(_rmsnorm(h, W['g_norm'][l, 0]), m[:, :, 0], m[:, :, 1])
    h = h + 0.5 * m[:, :, 2] * _swiglu(z, W['ffn_w_gate'][l, 0], W['ffn_w_up'][l, 0], W['ffn_w_down'][l, 0])
    z = _modulate(_rmsnorm(h, W['g_norm'][l, 1]), m[:, :, 3], m[:, :, 4])
    proj = z @ W['w_in'][l]
    q, k, v, u_s, v_s, f, gates = jnp.split(proj, IN_SPLITS, axis=-1)
    q = q.reshape(B, T, N_HEADS, HEAD_DIM)
    k = k.reshape(B, T, N_HEADS, HEAD_DIM)
    v = v.reshape(B, T, N_HEADS, HEAD_DIM)
    o_a = attend(q, k, v, W['rpb'][l]).reshape(B, T, ATTN_WIDTH)
    y_a = o_a @ W['p_attn'][l]
    o_b = _chunk_sgu(jax.nn.gelu(u_s), jax.nn.gelu(v_s), W['sgu_norm'][l], W['sgu_w'][l], W['sgu_b'][l])
    y_b = o_b @ W['p_sgu'][l]
    y_c = _fourier_mix(f) @ W['p_fnet'][l]
    g = jax.nn.sigmoid(gates.reshape(B, T, N_BRANCH, D_MODEL) + W['b_gate'][l])
    mixed = (g[:, :, 0] * y_a + g[:, :, 1] * y_b + g[:, :, 2] * y_c) @ W['w_out'][l]
    h = h + m[:, :, 5] * mixed
    z = _modulate(_rmsnorm(h, W['g_norm'][l, 2]), m[:, :, 6], m[:, :, 7])
    h = h + 0.5 * m[:, :, 8] * _swiglu(z, W['ffn_w_gate'][l, 1], W['ffn_w_up'][l, 1], W['ffn_w_down'][l, 1])
    return h, k, v


def setup_inputs(seed: int = 0) -> dict:
    key = jax.random.key(seed)
    ks = jax.random.split(key, 24)

    def nrm(k, shape, s):
        return jax.random.normal(k, shape, jnp.float32) * s

    return {
        'x_prompt': nrm(ks[0], (BATCH, SEQ, D_MODEL), 1.0),
        'x_sample': nrm(ks[1], (DEC_BATCH, DEC_SEQ, D_MODEL), 1.0),
        'cache_k': nrm(ks[2], (DEC_BATCH, DEPTH, PAST_LEN, N_HEADS, HEAD_DIM), 1.0),
        'cache_v': nrm(ks[3], (DEC_BATCH, DEPTH, PAST_LEN, N_HEADS, HEAD_DIM), 1.0),
        'c': nrm(ks[4], (DEC_BATCH, D_MODEL), 1.0),
        'c_ctx': nrm(ks[5], (D_MODEL,), 1.0),
        'w_mod': nrm(ks[6], (DEPTH, D_MODEL, N_MOD * D_MODEL), MOD_SCALE * D_MODEL ** -0.5),
        'b_mod': nrm(ks[7], (DEPTH, N_MOD * D_MODEL), 0.02),
        'g_norm': 1.0 + nrm(ks[8], (DEPTH, 3, D_MODEL), 0.02),
        'ffn_w_gate': nrm(ks[9], (DEPTH, 2, D_MODEL, D_FF), D_MODEL ** -0.5),
        'ffn_w_up': nrm(ks[10], (DEPTH, 2, D_MODEL, D_FF), D_MODEL ** -0.5),
        'ffn_w_down': nrm(ks[11], (DEPTH, 2, D_FF, D_MODEL), D_FF ** -0.5),
        'w_in': nrm(ks[12], (DEPTH, D_MODEL, IN_COLS), D_MODEL ** -0.5),
        'b_gate': nrm(ks[13], (DEPTH, N_BRANCH, D_MODEL), 0.02),
        'rpb': nrm(ks[14], (DEPTH, N_HEADS, 2 * WIN_H - 1, 2 * WIN_W - 1), 0.1),
        'sgu_norm': 1.0 + nrm(ks[15], (DEPTH, SGU_WIDTH), 0.02),
        'sgu_w': nrm(ks[16], (DEPTH, SGU_GROUPS, CHUNK, CHUNK), CHUNK ** -0.5),
        'sgu_b': nrm(ks[17], (DEPTH, SGU_GROUPS, CHUNK), 0.02),
        'p_attn': nrm(ks[18], (DEPTH, ATTN_WIDTH, D_MODEL), ATTN_WIDTH ** -0.5),
        'p_sgu': nrm(ks[19], (DEPTH, SGU_WIDTH, D_MODEL), SGU_WIDTH ** -0.5),
        'p_fnet': nrm(ks[20], (DEPTH, FNET_WIDTH, D_MODEL), FNET_WIDTH ** -0.5),
        'w_out': nrm(ks[21], (DEPTH, D_MODEL, D_MODEL), D_MODEL ** -0.5),
        'g_final': 1.0 + nrm(ks[22], (D_MODEL,), 0.02),
    }


def reference(x_prompt, x_sample, cache_k, cache_v, c, c_ctx, w_mod, b_mod, g_norm,
              ffn_w_gate, ffn_w_up, ffn_w_down, w_in, b_gate, rpb, sgu_norm, sgu_w, sgu_b,
              p_attn, p_sgu, p_fnet, w_out, g_final):
    W = {'w_mod': w_mod, 'b_mod': b_mod, 'g_norm': g_norm, 'ffn_w_gate': ffn_w_gate,
         'ffn_w_up': ffn_w_up, 'ffn_w_down': ffn_w_down, 'w_in': w_in, 'b_gate': b_gate,
         'rpb': rpb, 'sgu_norm': sgu_norm, 'sgu_w': sgu_w, 'sgu_b': sgu_b, 'p_attn': p_attn,
         'p_sgu': p_sgu, 'p_fnet': p_fnet, 'w_out': w_out}

    h = x_prompt
    ctx_cond = c_ctx[None, :]
    new_k, new_v = [], []
    for l in range(DEPTH):
        h, k_l, v_l = _layer(h, ctx_cond, l, W, _context_attention)
        new_k.append(k_l)
        new_v.append(v_l)
    y_prompt = _rmsnorm(h, g_final)
    new_cache_k = jnp.stack(new_k, axis=1)
    new_cache_v = jnp.stack(new_v, axis=1)

    h = x_sample
    for l in range(DEPTH):
        attend = functools.partial(_natten_latent, kc=cache_k[:, l], vc=cache_v[:, l])
        h, _, _ = _layer(h, c, l, W, attend)
    y_sample = _rmsnorm(h, g_final)

    return (y_prompt, y_sample, new_cache_k, new_cache_v)
```

```python
import functools

import numpy as np
import jax
import jax.numpy as jnp
from jax import lax
from jax.experimental import pallas as pl
from jax.experimental.pallas import tpu as pltpu

D_MODEL = 1024
BATCH = 32
SEQ = 256
DEPTH = 4
DEC_BATCH = 2
DEC_SEQ = 2048
PAST_LEN = 256
GRID_W = 64
GRID_ROWS = DEC_SEQ // GRID_W
WIN_H = 8
WIN_W = 16
HEAD_DIM = 64
ATTN_WIDTH = 512
N_HEADS = 8
SGU_WIDTH = 256
SGU_GROUPS = 4
SGU_GROUP_DIM = 64
CHUNK = 128
FNET_WIDTH = 256
FNET_GROUPS = 4
FNET_GROUP_DIM = 64
N_BRANCH = 3
D_FF = 2816
N_MOD = 9
RMS_EPS = 1e-6
NEG_INF = -1e30

CTX_TOKENS = BATCH * SEQ
LAT_TOKENS = DEC_BATCH * DEC_SEQ
TOKENS = CTX_TOKENS + LAT_TOKENS
N_COND = 1 + DEC_BATCH
COND_ROWS = 8
PROJ_COLS = 3 * ATTN_WIDTH + 2 * SGU_WIDTH + FNET_WIDTH
COL_BLOCK = 256
Q_COL, K_COL, V_COL = 0, 2, 4
US_COL, VS_COL, F_COL = 6, 7, 8

TM = 1024
N_TILES = TOKENS // TM
CTX_TILES = CTX_TOKENS // TM
TILES_PER_LAT = DEC_SEQ // TM
FF_CHUNK = 256
VMEM_LIMIT = 56 * 1024 * 1024

BF16 = jnp.bfloat16
F32 = jnp.float32


def _cond_row(i):
    return jnp.where(i < CTX_TILES, 0, 1 + (i - CTX_TILES) // TILES_PER_LAT)


def _params(n_axes):
    return pltpu.CompilerParams(dimension_semantics=("arbitrary",) * n_axes,
                                vmem_limit_bytes=VMEM_LIMIT)


def _dot(a, b):
    return jnp.dot(a, b, preferred_element_type=F32)


def _dot_nt(a, b):
    return lax.dot_general(a, b, (((1,), (1,)), ((), ())), preferred_element_type=F32)


def _rms_modulate(x, g, shift, scale):
    ms = jnp.mean(x * x, axis=-1, keepdims=True)
    y = x * lax.rsqrt(ms + RMS_EPS) * g
    return y * (1.0 + scale) + shift


def _silu(x):
    return x * (1.0 / (1.0 + jnp.exp(-x)))


def _sigmoid(x):
    return 1.0 / (1.0 + jnp.exp(-x))


def _gelu_tanh(x):
    c = np.float32(np.sqrt(2.0 / np.pi))
    return x * (0.5 * (1.0 + jnp.tanh(c * (x + np.float32(0.044715) * (x * x * x)))))


MOD_COLS = N_MOD * D_MODEL
MOD_BLOCK = 1152


def _mod_kernel(c_ref, w_ref, b_ref, o_ref):
    a = _silu(c_ref[...]).astype(BF16)
    o_ref[0] = _dot(a, w_ref[0].astype(BF16)) + b_ref[0]


def _modulation(cond, w_mod, b_mod):
    return pl.pallas_call(
        _mod_kernel,
        out_shape=jax.ShapeDtypeStruct((DEPTH, COND_ROWS, MOD_COLS), F32),
        grid=(DEPTH, MOD_COLS // MOD_BLOCK),
        in_specs=[
            pl.BlockSpec((COND_ROWS, D_MODEL), lambda l, j: (0, 0)),
            pl.BlockSpec((1, D_MODEL, MOD_BLOCK), lambda l, j: (l, 0, j)),
            pl.BlockSpec((1, 1, MOD_BLOCK), lambda l, j: (l, 0, j)),
        ],
        out_specs=pl.BlockSpec((1, COND_ROWS, MOD_BLOCK), lambda l, j: (l, 0, j)),
        compiler_params=_params(2),
        name="modulation",
    )(cond, w_mod, b_mod.reshape(DEPTH, 1, MOD_COLS))


def _ffn_kernel(h_ref, m_ref, g_ref, wg_ref, wu_ref, wd_ref, gf_ref, o_ref, z_ref, acc_ref, *,
                mod_base, final_norm):
    x = h_ref[...]
    shift = m_ref[0, mod_base:mod_base + 1, :]
    scale = m_ref[0, mod_base + 1:mod_base + 2, :]
    gate = m_ref[0, mod_base + 2:mod_base + 3, :]
    z_ref[...] = _rms_modulate(x, g_ref[...], shift, scale).astype(BF16)
    for j in range(D_FF // FF_CHUNK):
        cols = slice(j * FF_CHUNK, (j + 1) * FF_CHUNK)
        z = z_ref[...]
        a = _dot(z, wg_ref[:, cols])
        b = _dot(z, wu_ref[:, cols])
        act = (_silu(a) * b).astype(BF16)
        part = _dot(act, wd_ref[cols, :])
        if j == 0:
            acc_ref[...] = part
        else:
            acc_ref[...] += part
    y = x + (0.5 * gate) * acc_ref[...]
    if final_norm:
        ms = jnp.mean(y * y, axis=-1, keepdims=True)
        y = y * lax.rsqrt(ms + RMS_EPS) * gf_ref[...]
    o_ref[...] = y


def _ffn(h, m_l, g, wg, wu, wd, g_final, *, mod_base, final_norm):
    kern = functools.partial(_ffn_kernel, mod_base=mod_base, final_norm=final_norm)
    whole = lambda i: (0, 0)
    return pl.pallas_call(
        kern,
        out_shape=jax.ShapeDtypeStruct((TOKENS, D_MODEL), F32),
        grid=(N_TILES,),
        in_specs=[
            pl.BlockSpec((TM, D_MODEL), lambda i: (i, 0)),
            pl.BlockSpec((1, N_MOD, D_MODEL), lambda i: (_cond_row(i), 0, 0)),
            pl.BlockSpec((1, D_MODEL), whole),
            pl.BlockSpec((D_MODEL, D_FF), whole, pipeline_mode=pl.Buffered(1)),
            pl.BlockSpec((D_MODEL, D_FF), whole, pipeline_mode=pl.Buffered(1)),
            pl.BlockSpec((D_FF, D_MODEL), whole, pipeline_mode=pl.Buffered(1)),
            pl.BlockSpec((1, D_MODEL), whole),
        ],
        out_specs=pl.BlockSpec((TM, D_MODEL), lambda i: (i, 0)),
        scratch_shapes=[pltpu.VMEM((TM, D_MODEL), BF16), pltpu.VMEM((TM, D_MODEL), F32)],
        compiler_params=_params(1),
        name="ffn",
    )(h, m_l, g, wg, wu, wd, g_final)


def _inproj_kernel(h_ref, m_ref, g_ref, w_ref, o_ref):
    shift = m_ref[0, 3:4, :]
    scale = m_ref[0, 4:5, :]
    z = _rms_modulate(h_ref[...], g_ref[...], shift, scale).astype(BF16)
    o_ref[...] = _dot(z, w_ref[...])


def _inproj(h, m_l, g, w_in_proj):
    whole = lambda i: (0, 0)
    return pl.pallas_call(
        _inproj_kernel,
        out_shape=jax.ShapeDtypeStruct((TOKENS, PROJ_COLS), F32),
        grid=(N_TILES,),
        in_specs=[
            pl.BlockSpec((TM, D_MODEL), lambda i: (i, 0)),
            pl.BlockSpec((1, N_MOD, D_MODEL), lambda i: (_cond_row(i), 0, 0)),
            pl.BlockSpec((1, D_MODEL), whole),
            pl.BlockSpec((D_MODEL, PROJ_COLS), whole, pipeline_mode=pl.Buffered(1)),
        ],
        out_specs=pl.BlockSpec((TM, PROJ_COLS), lambda i: (i, 0)),
        compiler_params=_params(1),
        name="inproj",
    )(h, m_l, g, w_in_proj)


CTX_ATTN_BATCHES = TM // SEQ


def _softmax_rows(s):
    mx = jnp.max(s, axis=-1, keepdims=True)
    e = jnp.exp(s - mx)
    return e / jnp.sum(e, axis=-1, keepdims=True)


def _ctx_attn_kernel(q_ref, k_ref, v_ref, o_ref):
    scale = np.float32(HEAD_DIM ** -0.5)

    def one_request(b, carry):
        rows = pl.ds(pl.multiple_of(b * SEQ, SEQ), SEQ)
        q = q_ref[rows, :].astype(BF16)
        k = k_ref[rows, :].astype(BF16)
        v = v_ref[rows, :].astype(BF16)
        outs = []
        for h in range(N_HEADS):
            cols = slice(h * HEAD_DIM, (h + 1) * HEAD_DIM)
            s = _dot_nt(q[:, cols], k[:, cols]) * scale
            p = _softmax_rows(s).astype(BF16)
            outs.append(_dot(p, v[:, cols]))
        o_ref[rows, :] = jnp.concatenate(outs, axis=-1).astype(BF16)
        return carry

    lax.fori_loop(0, CTX_ATTN_BATCHES, one_request, 0)


def _ctx_attention(proj):
    spec = lambda c: pl.BlockSpec((TM, ATTN_WIDTH), lambda i, c=c: (i, c))
    return pl.pallas_call(
        _ctx_attn_kernel,
        out_shape=jax.ShapeDtypeStruct((CTX_TOKENS, ATTN_WIDTH), BF16),
        grid=(CTX_TILES,),
        in_specs=[spec(0), spec(1), spec(2)],
        out_specs=pl.BlockSpec((TM, ATTN_WIDTH), lambda i: (i, 0)),
        compiler_params=_params(1),
        name="ctx_attention",
    )(proj, proj, proj)


N_DROW = 2 * WIN_H - 1
N_DCOL = 2 * WIN_W - 1
N_DROW_PAIRS = N_DROW - 1
KEY_ROWS = WIN_H
KEYS_LOCAL = KEY_ROWS * GRID_W


def _rpb_table_kernel(rpb_ref, o_ref):
    l = pl.program_id(0)
    h = pl.program_id(1)
    q = lax.broadcasted_iota(jnp.int32, (GRID_W, 2 * GRID_W), 0)
    n = lax.broadcasted_iota(jnp.int32, (GRID_W, 2 * GRID_W), 1)
    second = n >= GRID_W
    diff = jnp.where(second, n - GRID_W, n) - q + (WIN_W - 1)
    base = (l * N_HEADS + h) * (N_DROW * N_DCOL)

    def one_pair(d, carry):
        t = jnp.zeros((GRID_W, 2 * GRID_W), F32)
        for dd in range(N_DCOL):
            lo = rpb_ref[base + d * N_DCOL + dd]
            hi = rpb_ref[base + (d + 1) * N_DCOL + dd]
            t = jnp.where(diff == dd, jnp.where(second, hi, lo), t)
        o_ref[0, 0, d] = t
        return carry

    lax.fori_loop(0, N_DROW_PAIRS, one_pair, 0)


def _rpb_table(rpb):
    return pl.pallas_call(
        _rpb_table_kernel,
        out_shape=jax.ShapeDtypeStruct((DEPTH, N_HEADS, N_DROW_PAIRS, GRID_W, 2 * GRID_W), F32),
        grid=(DEPTH, N_HEADS),
        in_specs=[pl.BlockSpec(memory_space=pltpu.SMEM)],
        out_specs=pl.BlockSpec((1, 1, N_DROW_PAIRS, GRID_W, 2 * GRID_W), lambda l, h: (l, h, 0, 0, 0)),
        compiler_params=_params(2),
        name="rpb_table",
    )(rpb.reshape(-1))


def _natten_kernel(q_ref, k_ref, v_ref, kc_ref, vc_ref, tab_ref, o_ref, qb_ref, kb_ref, vb_ref):
    scale = np.float32(HEAD_DIM ** -0.5)
    qb_ref[...] = q_ref[...].astype(BF16)
    kb_ref[...] = k_ref[...].astype(BF16)
    vb_ref[...] = v_ref[...].astype(BF16)
    kc = kc_ref[0, 0].astype(BF16)
    vc = vc_ref[0, 0].astype(BF16)
    qcol = lax.broadcasted_iota(jnp.int32, (GRID_W, KEYS_LOCAL), 0)
    kcol = lax.broadcasted_iota(jnp.int32, (GRID_W, KEYS_LOCAL), 1) % GRID_W
    start = jnp.clip(qcol - WIN_W // 2, 0, GRID_W - WIN_W)
    col_ok = (kcol >= start) & (kcol < start + WIN_W)

    def one_row(r, carry):
        r0 = jnp.clip(r - KEY_ROWS // 2, 0, GRID_ROWS - KEY_ROWS)
        d0 = r0 - r + (WIN_H - 1)
        qrows = pl.ds(pl.multiple_of(r * GRID_W, GRID_W), GRID_W)
        krows = pl.ds(pl.multiple_of(r0 * GRID_W, GRID_W), KEYS_LOCAL)
        q = qb_ref[qrows, :]
        k = kb_ref[krows, :]
        v = vb_ref[krows, :]
        outs = []
        for h in range(N_HEADS):
            cols = slice(h * HEAD_DIM, (h + 1) * HEAD_DIM)
            bias = jnp.concatenate([tab_ref[0, h, d0 + 2 * i] for i in range(KEY_ROWS // 2)], axis=-1)
            s_loc = _dot_nt(q[:, cols], k[:, cols]) * scale + bias
            s_loc = jnp.where(col_ok, s_loc, NEG_INF)
            s_ctx = _dot_nt(q[:, cols], kc[:, cols]) * scale
            mx = jnp.maximum(jnp.max(s_loc, axis=-1, keepdims=True), jnp.max(s_ctx, axis=-1, keepdims=True))
            e_loc = jnp.exp(s_loc - mx)
            e_ctx = jnp.exp(s_ctx - mx)
            den = jnp.sum(e_loc, axis=-1, keepdims=True) + jnp.sum(e_ctx, axis=-1, keepdims=True)
            p_loc = (e_loc / den).astype(BF16)
            p_ctx = (e_ctx / den).astype(BF16)
            outs.append(_dot(p_loc, v[:, cols]) + _dot(p_ctx, vc[:, cols]))
        o_ref[qrows, :] = jnp.concatenate(outs, axis=-1).astype(BF16)
        return carry

    lax.fori_loop(0, GRID_ROWS, one_row, 0)


def _natten(proj, cache_k, cache_v, table, layer):
    lat0 = CTX_TOKENS // DEC_SEQ
    spec = lambda c: pl.BlockSpec((DEC_SEQ, ATTN_WIDTH), lambda b, c=c: (lat0 + b, c))
    cache_spec = pl.BlockSpec((1, 1, PAST_LEN, ATTN_WIDTH), lambda b: (b, layer, 0, 0))
    return pl.pallas_call(
        _natten_kernel,
        out_shape=jax.ShapeDtypeStruct((LAT_TOKENS, ATTN_WIDTH), BF16),
        grid=(DEC_BATCH,),
        in_specs=[spec(0), spec(1), spec(2), cache_spec, cache_spec,
                  pl.BlockSpec((1, N_HEADS, N_DROW_PAIRS, GRID_W, 2 * GRID_W), lambda b: (layer, 0, 0, 0, 0))],
        out_specs=pl.BlockSpec((DEC_SEQ, ATTN_WIDTH), lambda b: (b, 0)),
        scratch_shapes=[pltpu.VMEM((DEC_SEQ, ATTN_WIDTH), BF16)] * 3,
        compiler_params=_params(1),
        name="natten",
    )(proj, proj, proj, cache_k, cache_v, table)


def _sgu_kernel(u_ref, v_ref, g_ref, w_ref, b_ref, o_ref):
    u = _gelu_tanh(u_ref[...])
    v = _gelu_tanh(v_ref[...])
    ms = jnp.mean(v * v, axis=-1, keepdims=True)
    vn = v * lax.rsqrt(ms + RMS_EPS) * g_ref[...]
    group = lax.broadcasted_iota(jnp.int32, (CHUNK, SGU_WIDTH), 1) // SGU_GROUP_DIM
    w = w_ref[...]
    bias = b_ref[...]
    for n in range(TM // CHUNK):
        rows = slice(n * CHUNK, (n + 1) * CHUNK)
        x = vn[rows, :]
        stacked = jnp.concatenate(
            [jnp.where(group == g, x, 0.0) for g in range(SGU_GROUPS)], axis=0).astype(BF16)
        mixed = _dot(w, stacked) + bias
        o_ref[rows, :] = (u[rows, :] * mixed).astype(BF16)


def _sgu(proj, sgu_norm, w_cat, bias_tile):
    whole = lambda i: (0, 0)
    return pl.pallas_call(
        _sgu_kernel,
        out_shape=jax.ShapeDtypeStruct((TOKENS, SGU_WIDTH), BF16),
        grid=(N_TILES,),
        in_specs=[
            pl.BlockSpec((TM, SGU_WIDTH), lambda i: (i, US_COL)),
            pl.BlockSpec((TM, SGU_WIDTH), lambda i: (i, VS_COL)),
            pl.BlockSpec((1, SGU_WIDTH), whole),
            pl.BlockSpec((CHUNK, SGU_GROUPS * CHUNK), whole),
            pl.BlockSpec((CHUNK, SGU_WIDTH), whole),
        ],
        out_specs=pl.BlockSpec((TM, SGU_WIDTH), lambda i: (i, 0)),
        compiler_params=_params(1),
        name="sgu",
    )(proj, proj, sgu_norm, w_cat, bias_tile)


def _dft_tables(n):
    k = np.arange(n, dtype=np.int64)
    ang = 2.0 * np.pi * ((k[:, None] * k[None, :]) % n).astype(np.float64) / n
    return np.cos(ang), np.sin(ang)


def _channel_tables():
    c, s = _dft_tables(FNET_GROUP_DIM)
    eye = np.eye(FNET_GROUPS)
    return np.concatenate([np.kron(eye, c), np.kron(eye, s)], axis=1)


_CHAN_TABLE = _channel_tables().astype(np.float32)
_CTX_POS_TABLE = np.concatenate(_dft_tables(SEQ), axis=1).astype(np.float32)
_LAT_POS_TABLE = np.concatenate(_dft_tables(DEC_SEQ), axis=1).astype(np.float32)
FNET_CTX_SCALE = np.float32(1.0 / np.sqrt(SEQ * FNET_GROUP_DIM))
FNET_LAT_SCALE = np.float32(1.0 / np.sqrt(DEC_SEQ * FNET_GROUP_DIM))


def _fnet_ctx_kernel(f_ref, chan_ref, pos_ref, o_ref):
    y = _dot(f_ref[...].astype(BF16), chan_ref[...].astype(BF16))
    pos = pos_ref[...].astype(BF16)
    for b in range(TM // SEQ):
        rows = slice(b * SEQ, (b + 1) * SEQ)
        yc = y[rows, :FNET_WIDTH].astype(BF16)
        ys = y[rows, FNET_WIDTH:].astype(BF16)
        out = _dot(pos[:, :SEQ], yc) - _dot(pos[:, SEQ:], ys)
        o_ref[rows, :] = (out * FNET_CTX_SCALE).astype(BF16)


def _fnet_ctx(proj, chan, pos):
    whole = lambda i: (0, 0)
    return pl.pallas_call(
        _fnet_ctx_kernel,
        out_shape=jax.ShapeDtypeStruct((CTX_TOKENS, FNET_WIDTH), BF16),
        grid=(CTX_TILES,),
        in_specs=[
            pl.BlockSpec((TM, FNET_WIDTH), lambda i: (i, F_COL)),
            pl.BlockSpec((FNET_WIDTH, 2 * FNET_WIDTH), whole),
            pl.BlockSpec((SEQ, 2 * SEQ), whole),
        ],
        out_specs=pl.BlockSpec((TM, FNET_WIDTH), lambda i: (i, 0)),
        compiler_params=_params(1),
        name="fnet_ctx",
    )(proj, chan, pos)


FNET_LAT_ROWS = 256


def _fnet_lat_kernel(f0_ref, f1_ref, chan_ref, pos_ref, o_ref, yc_ref, ys_ref):
    @pl.when(pl.program_id(0) == 0)
    def _():
        for b, f_ref in enumerate((f0_ref, f1_ref)):
            y = _dot(f_ref[...].astype(BF16), chan_ref[...].astype(BF16))
            cols = slice(b * FNET_WIDTH, (b + 1) * FNET_WIDTH)
            yc_ref[:, cols] = y[:, :FNET_WIDTH].astype(BF16)
            ys_ref[:, cols] = y[:, FNET_WIDTH:].astype(BF16)

    out = (_dot(pos_ref[:, :DEC_SEQ].astype(BF16), yc_ref[...])
           - _dot(pos_ref[:, DEC_SEQ:].astype(BF16), ys_ref[...]))
    out = out * FNET_LAT_SCALE
    for b in range(DEC_BATCH):
        o_ref[b] = out[:, b * FNET_WIDTH:(b + 1) * FNET_WIDTH].astype(BF16)


def _fnet_lat(proj, chan, pos):
    lat0 = CTX_TOKENS // DEC_SEQ
    return pl.pallas_call(
        _fnet_lat_kernel,
        out_shape=jax.ShapeDtypeStruct((DEC_BATCH, DEC_SEQ, FNET_WIDTH), BF16),
        grid=(DEC_SEQ // FNET_LAT_ROWS,),
        in_specs=[
            pl.BlockSpec((DEC_SEQ, FNET_WIDTH), lambda j: (lat0, F_COL)),
            pl.BlockSpec((DEC_SEQ, FNET_WIDTH), lambda j: (lat0 + 1, F_COL)),
            pl.BlockSpec((FNET_WIDTH, 2 * FNET_WIDTH), lambda j: (0, 0)),
            pl.BlockSpec((FNET_LAT_ROWS, 2 * DEC_SEQ), lambda j: (j, 0)),
        ],
        out_specs=pl.BlockSpec((DEC_BATCH, FNET_LAT_ROWS, FNET_WIDTH), lambda j: (0, j, 0)),
        scratch_shapes=[pltpu.VMEM((DEC_SEQ, DEC_BATCH * FNET_WIDTH), BF16)] * 2,
        compiler_params=_params(1),
        name="fnet_lat",
    )(proj, proj, chan, pos)


def _merge_kernel(h_ref, m_ref, g_ref, wg_ref, bg_ref, oac_ref, oal_ref, ob_ref, occ_ref, ocl_ref,
                  pa_ref, pb_ref, pc_ref, wo_ref, o_ref):
    ctx = pl.program_id(0) < CTX_TILES
    x = h_ref[...]
    shift = m_ref[0, 3:4, :]
    scale = m_ref[0, 4:5, :]
    gate = m_ref[0, 5:6, :]
    z = _rms_modulate(x, g_ref[...], shift, scale).astype(BF16)
    oa = jnp.where(ctx, oac_ref[...], oal_ref[...])
    oc = jnp.where(ctx, occ_ref[...], ocl_ref[...])
    branches = ((oa, pa_ref), (ob_ref[...], pb_ref), (oc, pc_ref))
    mix = None
    for j, (o, p_ref) in enumerate(branches):
        cols = slice(j * D_MODEL, (j + 1) * D_MODEL)
        gj = _sigmoid(_dot(z, wg_ref[:, cols]) + bg_ref[j:j + 1, :])
        term = gj * _dot(o, p_ref[...])
        mix = term if mix is None else mix + term
    o_ref[...] = x + gate * _dot(mix.astype(BF16), wo_ref[...])


def _merge(h, m_l, g, w_gates, b_gate, oa_ctx, oa_lat, ob, oc_ctx, oc_lat, p_attn, p_sgu, p_fnet, w_out):
    whole = lambda i: (0, 0)
    ctx_tile = lambda i: (jnp.minimum(i, CTX_TILES - 1), 0)
    lat_tile = lambda i: (jnp.maximum(i - CTX_TILES, 0), 0)
    single = dict(pipeline_mode=pl.Buffered(1))
    return pl.pallas_call(
        _merge_kernel,
        out_shape=jax.ShapeDtypeStruct((TOKENS, D_MODEL), F32),
        grid=(N_TILES,),
        in_specs=[
            pl.BlockSpec((TM, D_MODEL), lambda i: (i, 0)),
            pl.BlockSpec((1, N_MOD, D_MODEL), lambda i: (_cond_row(i), 0, 0)),
            pl.BlockSpec((1, D_MODEL), whole),
            pl.BlockSpec((D_MODEL, N_BRANCH * D_MODEL), whole, **single),
            pl.BlockSpec((N_BRANCH, D_MODEL), whole),
            pl.BlockSpec((TM, ATTN_WIDTH), ctx_tile),
            pl.BlockSpec((TM, ATTN_WIDTH), lat_tile),
            pl.BlockSpec((TM, SGU_WIDTH), lambda i: (i, 0)),
            pl.BlockSpec((TM, FNET_WIDTH), ctx_tile),
            pl.BlockSpec((TM, FNET_WIDTH), lat_tile),
            pl.BlockSpec((ATTN_WIDTH, D_MODEL), whole, **single),
            pl.BlockSpec((SGU_WIDTH, D_MODEL), whole, **single),
            pl.BlockSpec((FNET_WIDTH, D_MODEL), whole, **single),
            pl.BlockSpec((D_MODEL, D_MODEL), whole, **single),
        ],
        out_specs=pl.BlockSpec((TM, D_MODEL), lambda i: (i, 0)),
        compiler_params=_params(1),
        name="merge",
    )(h, m_l, g, w_gates, b_gate, oa_ctx, oa_lat, ob, oc_ctx, oc_lat, p_attn, p_sgu, p_fnet, w_out)


def kernel(x_prompt, x_sample, cache_k, cache_v, c, c_ctx, w_mod, b_mod, g_norm, ffn_w_gate, ffn_w_up,
           ffn_w_down, w_in, b_gate, rpb, sgu_norm, sgu_w, sgu_b, p_attn, p_sgu, p_fnet, w_out, g_final):
    h = jnp.concatenate([x_prompt.reshape(CTX_TOKENS, D_MODEL), x_sample.reshape(LAT_TOKENS, D_MODEL)], axis=0)
    cond = jnp.concatenate([c_ctx[None, :], c, jnp.zeros((COND_ROWS - N_COND, D_MODEL), F32)], axis=0)
    mod = _modulation(cond, w_mod, b_mod).reshape(DEPTH, COND_ROWS, N_MOD, D_MODEL)
    table = _rpb_table(rpb)
    cache_k = cache_k.reshape(DEC_BATCH, DEPTH, PAST_LEN, ATTN_WIDTH)
    cache_v = cache_v.reshape(DEC_BATCH, DEPTH, PAST_LEN, ATTN_WIDTH)

    chan = jnp.asarray(_CHAN_TABLE)
    pos_ctx = jnp.asarray(_CTX_POS_TABLE)
    pos_lat = jnp.asarray(_LAT_POS_TABLE)
    g_final2 = g_final.reshape(1, D_MODEL)

    new_k, new_v = [], []
    for l in range(DEPTH):
        m_l = mod[l]
        wg = ffn_w_gate[l].astype(BF16)
        wu = ffn_w_up[l].astype(BF16)
        wd = ffn_w_down[l].astype(BF16)
        w_in_l = w_in[l].astype(BF16)
        g_l = g_norm[l].reshape(3, 1, D_MODEL)

        h = _ffn(h, m_l, g_l[0], wg[0], wu[0], wd[0], g_final2, mod_base=0, final_norm=False)

        proj = _inproj(h, m_l, g_l[1], w_in_l[:, :PROJ_COLS])
        new_k.append(proj[:CTX_TOKENS, ATTN_WIDTH:2 * ATTN_WIDTH].reshape(BATCH, SEQ, N_HEADS, HEAD_DIM))
        new_v.append(proj[:CTX_TOKENS, 2 * ATTN_WIDTH:3 * ATTN_WIDTH].reshape(BATCH, SEQ, N_HEADS, HEAD_DIM))
        oa_ctx = _ctx_attention(proj)
        oa_lat = _natten(proj, cache_k, cache_v, table, l)
        w_cat = jnp.transpose(sgu_w[l], (1, 0, 2)).reshape(CHUNK, SGU_GROUPS * CHUNK).astype(BF16)
        bias_tile = jnp.repeat(jnp.transpose(sgu_b[l]), SGU_GROUP_DIM, axis=1)
        ob = _sgu(proj, sgu_norm[l].reshape(1, SGU_WIDTH), w_cat, bias_tile)
        oc_ctx = _fnet_ctx(proj, chan, pos_ctx)
        oc_lat = _fnet_lat(proj, chan, pos_lat).reshape(LAT_TOKENS, FNET_WIDTH)
        h = _merge(h, m_l, g_l[1], w_in_l[:, PROJ_COLS:], b_gate[l], oa_ctx, oa_lat, ob, oc_ctx, oc_lat,
                   p_attn[l].astype(BF16), p_sgu[l].astype(BF16), p_fnet[l].astype(BF16), w_out[l].astype(BF16))

        h = _ffn(h, m_l, g_l[2], wg[1], wu[1], wd[1], g_final2, mod_base=6, final_norm=(l == DEPTH - 1))

    y_prompt = h[:CTX_TOKENS].reshape(BATCH, SEQ, D_MODEL)
    y_sample = h[CTX_TOKENS:].reshape(DEC_BATCH, DEC_SEQ, D_MODEL)
    return (y_prompt, y_sample, jnp.stack(new_k, axis=1), jnp.stack(new_v, axis=1))
```

```python
import functools

import numpy as np
import jax
import jax.numpy as jnp
from jax import lax
from jax.experimental import pallas as pl
from jax.experimental.pallas import tpu as pltpu

D_MODEL = 1024
BATCH = 32
SEQ = 256
DEPTH = 4
DEC_BATCH = 2
DEC_SEQ = 2048
PAST_LEN = 256
GRID_W = 64
GRID_ROWS = DEC_SEQ // GRID_W
WIN_H = 8
WIN_W = 16
HEAD_DIM = 64
ATTN_WIDTH = 512
N_HEADS = 8
SGU_WIDTH = 256
SGU_GROUPS = 4
SGU_GROUP_DIM = 64
CHUNK = 128
FNET_WIDTH = 256
FNET_GROUPS = 4
FNET_GROUP_DIM = 64
N_BRANCH = 3
D_FF = 2816
N_MOD = 9
RMS_EPS = 1e-6
NEG_INF = -1e30

CTX_TOKENS = BATCH * SEQ
LAT_TOKENS = DEC_BATCH * DEC_SEQ
TOKENS = CTX_TOKENS + LAT_TOKENS
N_COND = 1 + DEC_BATCH
COND_ROWS = 8
PROJ_COLS = 3 * ATTN_WIDTH + 2 * SGU_WIDTH + FNET_WIDTH
GATE_COLS = N_BRANCH * D_MODEL
COL_BLOCK = 256
US_COL, VS_COL, F_COL = 6, 7, 8

TM = 1024
N_TILES = TOKENS // TM
CTX_TILES = CTX_TOKENS // TM
TILES_PER_LAT = DEC_SEQ // TM
REQ_PER_TILE = TM // SEQ
FF_CHUNK = 256
VMEM_LIMIT = 56 * 1024 * 1024

BF16 = jnp.bfloat16
F32 = jnp.float32
ATTN_SCALE = np.float32(HEAD_DIM ** -0.5)


def _cond_row(i):
    return jnp.where(i < CTX_TILES, 0, 1 + (i - CTX_TILES) // TILES_PER_LAT)


def _ctx_tile(i):
    return jnp.minimum(i, CTX_TILES - 1)


def _lat_tile(i):
    return jnp.maximum(i - CTX_TILES, 0)


def _params(n_axes):
    return pltpu.CompilerParams(dimension_semantics=("arbitrary",) * n_axes,
                                vmem_limit_bytes=VMEM_LIMIT)


def _dot(a, b):
    return jnp.dot(a, b, preferred_element_type=F32)


def _dot_nt(a, b):
    return lax.dot_general(a, b, (((1,), (1,)), ((), ())), preferred_element_type=F32)


def _rms_modulate(x, g, shift, scale):
    ms = jnp.mean(x * x, axis=-1, keepdims=True)
    y = x * lax.rsqrt(ms + RMS_EPS) * g
    return y * (1.0 + scale) + shift


def _silu(x):
    return x * (1.0 / (1.0 + jnp.exp(-x)))


def _sigmoid(x):
    return 1.0 / (1.0 + jnp.exp(-x))


def _gelu_tanh(x):
    c = np.float32(np.sqrt(2.0 / np.pi))
    return x * (0.5 * (1.0 + jnp.tanh(c * (x + np.float32(0.044715) * (x * x * x)))))


def _single(shape, index_map):
    return pl.BlockSpec(shape, index_map, pipeline_mode=pl.Buffered(1))


MOD_COLS = N_MOD * D_MODEL
MOD_BLOCK = 1152


def _mod_kernel(c_ref, w_ref, b_ref, o_ref):
    a = _silu(c_ref[...]).astype(BF16)
    o_ref[0] = _dot(a, w_ref[0].astype(BF16)) + b_ref[0]


def _modulation(cond, w_mod, b_mod):
    return pl.pallas_call(
        _mod_kernel,
        out_shape=jax.ShapeDtypeStruct((DEPTH, COND_ROWS, MOD_COLS), F32),
        grid=(DEPTH, MOD_COLS // MOD_BLOCK),
        in_specs=[
            pl.BlockSpec((COND_ROWS, D_MODEL), lambda l, j: (0, 0)),
            pl.BlockSpec((1, D_MODEL, MOD_BLOCK), lambda l, j: (l, 0, j)),
            pl.BlockSpec((1, 1, MOD_BLOCK), lambda l, j: (l, 0, j)),
        ],
        out_specs=pl.BlockSpec((1, COND_ROWS, MOD_BLOCK), lambda l, j: (l, 0, j)),
        compiler_params=_params(2),
        name="modulation",
    )(cond, w_mod, b_mod.reshape(DEPTH, 1, MOD_COLS))


def _mod_spec(layer):
    return pl.BlockSpec((None, 1, N_MOD, D_MODEL), lambda i: (layer, _cond_row(i), 0, 0))


def _gain_spec(layer, sub):
    return pl.BlockSpec((None, None, 1, D_MODEL), lambda i: (layer, sub, 0, 0))


def _ffn_kernel(*refs, mod_base, split_in, split_out):
    n_in = 2 if split_in else 1
    x_refs, (m_ref, g_ref, wg_ref, wu_ref, wd_ref, gf_ref) = refs[:n_in], refs[n_in:n_in + 6]
    n_out = 2 if split_out else 1
    o_refs = refs[n_in + 6:n_in + 6 + n_out]
    z_ref, acc_ref = refs[n_in + 6 + n_out:]
    is_ctx = pl.program_id(0) < CTX_TILES

    x = jnp.where(is_ctx, x_refs[0][...], x_refs[1][...]) if split_in else x_refs[0][...]
    shift = m_ref[0, mod_base:mod_base + 1, :]
    scale = m_ref[0, mod_base + 1:mod_base + 2, :]
    gate = m_ref[0, mod_base + 2:mod_base + 3, :]
    z_ref[...] = _rms_modulate(x, g_ref[...], shift, scale).astype(BF16)
    for j in range(D_FF // FF_CHUNK):
        cols = slice(j * FF_CHUNK, (j + 1) * FF_CHUNK)
        z = z_ref[...]
        a = _dot(z, wg_ref[:, cols])
        b = _dot(z, wu_ref[:, cols])
        act = (_silu(a) * b).astype(BF16)
        part = _dot(act, wd_ref[cols, :])
        if j == 0:
            acc_ref[...] = part
        else:
            acc_ref[...] += part
    y = x + (0.5 * gate) * acc_ref[...]
    if not split_out:
        o_refs[0][...] = y
    else:
        ms = jnp.mean(y * y, axis=-1, keepdims=True)
        y = y * lax.rsqrt(ms + RMS_EPS) * gf_ref[...]

        @pl.when(is_ctx)
        def _():
            o_refs[0][...] = y

        @pl.when(jnp.logical_not(is_ctx))
        def _():
            o_refs[1][...] = y


def _ffn(xs, mod, gains, wg, wu, wd, g_final, layer, sub, *, split_in=False, split_out=False):
    kern = functools.partial(_ffn_kernel, mod_base=3 * sub if sub == 0 else 6, split_in=split_in, split_out=split_out)
    row = pl.BlockSpec((TM, D_MODEL), lambda i: (i, 0))
    ctx_row = pl.BlockSpec((TM, D_MODEL), lambda i: (_ctx_tile(i), 0))
    lat_row = pl.BlockSpec((TM, D_MODEL), lambda i: (_lat_tile(i), 0))
    ffn_sub = 0 if sub == 0 else 1
    w_idx = lambda i: (layer, ffn_sub, 0, 0)
    if split_out:
        out_shape = (jax.ShapeDtypeStruct((CTX_TOKENS, D_MODEL), F32), jax.ShapeDtypeStruct((LAT_TOKENS, D_MODEL), F32))
        out_specs = (ctx_row, lat_row)
    else:
        out_shape = jax.ShapeDtypeStruct((TOKENS, D_MODEL), F32)
        out_specs = row
    return pl.pallas_call(
        kern,
        out_shape=out_shape,
        grid=(N_TILES,),
        in_specs=([ctx_row, lat_row] if split_in else [row]) + [
            _mod_spec(layer),
            _gain_spec(layer, sub),
            _single((None, None, D_MODEL, D_FF), w_idx),
            _single((None, None, D_MODEL, D_FF), w_idx),
            _single((None, None, D_FF, D_MODEL), w_idx),
            pl.BlockSpec((1, D_MODEL), lambda i: (0, 0)),
        ],
        out_specs=out_specs,
        scratch_shapes=[pltpu.VMEM((TM, D_MODEL), BF16), pltpu.VMEM((TM, D_MODEL), F32)],
        compiler_params=_params(1),
        name="ffn",
    )(*xs, mod, gains, wg, wu, wd, g_final)


def _inproj_kernel(h_ref, m_ref, g_ref, w_ref, kt_in, vt_in, p_ref, kt_ref, vt_ref):
    del kt_in, vt_in
    shift = m_ref[0, 3:4, :]
    scale = m_ref[0, 4:5, :]
    z = _rms_modulate(h_ref[...], g_ref[...], shift, scale).astype(BF16)
    p_ref[...] = _dot(z, w_ref[...])

    @pl.when(pl.program_id(0) < CTX_TILES)
    def _():
        for b in range(REQ_PER_TILE):
            rows = slice(b * SEQ, (b + 1) * SEQ)
            kt_ref[b] = p_ref[rows, ATTN_WIDTH:2 * ATTN_WIDTH].T
            vt_ref[b] = p_ref[rows, 2 * ATTN_WIDTH:3 * ATTN_WIDTH].T


def _inproj(h, mod, gains, w_proj, kt, vt, layer):
    cache_shape = jax.ShapeDtypeStruct((BATCH, DEPTH, ATTN_WIDTH, SEQ), F32)
    cache_spec = pl.BlockSpec((REQ_PER_TILE, None, ATTN_WIDTH, SEQ), lambda i: (_ctx_tile(i), layer, 0, 0))
    return pl.pallas_call(
        _inproj_kernel,
        out_shape=(jax.ShapeDtypeStruct((TOKENS, PROJ_COLS), F32), cache_shape, cache_shape),
        grid=(N_TILES,),
        in_specs=[
            pl.BlockSpec((TM, D_MODEL), lambda i: (i, 0)),
            _mod_spec(layer),
            _gain_spec(layer, 1),
            _single((None, D_MODEL, PROJ_COLS), lambda i: (layer, 0, 0)),
            pl.BlockSpec(memory_space=pl.ANY),
            pl.BlockSpec(memory_space=pl.ANY),
        ],
        out_specs=(pl.BlockSpec((TM, PROJ_COLS), lambda i: (i, 0)), cache_spec, cache_spec),
        input_output_aliases={4: 1, 5: 2},
        compiler_params=_params(1),
        name="inproj",
    )(h, mod, gains, w_proj, kt, vt)


HEAD_PAIRS = N_HEADS // 2
PAIR_WIDTH = 2 * HEAD_DIM
PAIR_COLS = [slice(j * PAIR_WIDTH, (j + 1) * PAIR_WIDTH) for j in range(HEAD_PAIRS)]


def _first_of_pair(rows):
    return lax.broadcasted_iota(jnp.int32, (rows, PAIR_WIDTH), 1) < HEAD_DIM


def _pair_queries(q):
    first = _first_of_pair(q.shape[0])
    zero = jnp.zeros_like(q)
    return jnp.concatenate([jnp.where(first, q, zero), jnp.where(first, zero, q)], axis=0)


def _pair_outputs(o):
    m = o.shape[0] // 2
    return jnp.where(_first_of_pair(m), o[:m], o[m:])


def _ctx_attn_kernel(q_ref, kt_ref, v_ref, o_ref):
    def one_request(b, carry):
        rows = pl.ds(pl.multiple_of(b * SEQ, SEQ), SEQ)
        q = (q_ref[rows, :] * ATTN_SCALE).astype(BF16)
        kt = kt_ref[b].astype(BF16)
        v = v_ref[rows, :].astype(BF16)
        s = jnp.concatenate([_dot(_pair_queries(q[:, c]), kt[c, :]) for c in PAIR_COLS], axis=0)
        e = jnp.exp(s - jnp.max(s, axis=-1, keepdims=True))
        inv = 1.0 / jnp.sum(e, axis=-1, keepdims=True)
        p = e.astype(BF16)
        outs = []
        for j, c in enumerate(PAIR_COLS):
            pair = slice(2 * j * SEQ, 2 * (j + 1) * SEQ)
            outs.append(_pair_outputs(_dot(p[pair, :], v[:, c]) * inv[pair, :]))
        o_ref[rows, :] = jnp.concatenate(outs, axis=-1).astype(BF16)
        return carry

    lax.fori_loop(0, REQ_PER_TILE, one_request, 0, unroll=2)


def _ctx_attention(proj, kt, layer):
    return pl.pallas_call(
        _ctx_attn_kernel,
        out_shape=jax.ShapeDtypeStruct((CTX_TOKENS, ATTN_WIDTH), BF16),
        grid=(CTX_TILES,),
        in_specs=[
            pl.BlockSpec((TM, ATTN_WIDTH), lambda i: (i, 0)),
            pl.BlockSpec((REQ_PER_TILE, None, ATTN_WIDTH, SEQ), lambda i: (i, layer, 0, 0)),
            pl.BlockSpec((TM, ATTN_WIDTH), lambda i: (i, 2)),
        ],
        out_specs=pl.BlockSpec((TM, ATTN_WIDTH), lambda i: (i, 0)),
        compiler_params=_params(1),
        name="ctx_attention",
    )(proj, kt, proj)


N_DROW = 2 * WIN_H - 1
N_DCOL = 2 * WIN_W - 1
N_DROW_PAIRS = N_DROW - 1
KEY_ROWS = WIN_H
KEYS_LOCAL = KEY_ROWS * GRID_W
RPB_LANES = 2 * GRID_W


def _build_bias_table(rpb_ref, tab_ref):
    first = _first_of_pair(GRID_W)
    for h in range(N_HEADS):
        for d in range(N_DROW_PAIRS):
            lo = jnp.broadcast_to(rpb_ref[h, d:d + 1, :], (GRID_W, RPB_LANES))
            hi = jnp.broadcast_to(rpb_ref[h, d + 1:d + 2, :], (GRID_W, RPB_LANES))
            lo = pltpu.roll(lo, RPB_LANES - (WIN_W - 1), 1, stride=1, stride_axis=0)
            hi = pltpu.roll(hi, GRID_W - (WIN_W - 1), 1, stride=1, stride_axis=0)
            tab_ref[h, d] = jnp.where(first, lo, hi)


def _natten_kernel(q_ref, k_ref, v_ref, kc_ref, vc_ref, rpb_ref, o_ref, qb_ref, kb_ref, vb_ref, tab_ref):
    @pl.when(pl.program_id(0) == 0)
    def _():
        _build_bias_table(rpb_ref, tab_ref)

    qb_ref[...] = (q_ref[...] * ATTN_SCALE).astype(BF16)
    kb_ref[...] = k_ref[...].astype(BF16)
    vb_ref[...] = v_ref[...].astype(BF16)
    kc = kc_ref[...].astype(BF16)
    vc = vc_ref[...].astype(BF16)
    qcol = lax.broadcasted_iota(jnp.int32, (N_HEADS * GRID_W, KEYS_LOCAL), 0) % GRID_W
    kcol = lax.broadcasted_iota(jnp.int32, (N_HEADS * GRID_W, KEYS_LOCAL), 1) % GRID_W
    start = jnp.clip(qcol - WIN_W // 2, 0, GRID_W - WIN_W)
    col_ok = (kcol >= start) & (kcol < start + WIN_W)

    def one_row(r, carry):
        r0 = jnp.clip(r - KEY_ROWS // 2, 0, GRID_ROWS - KEY_ROWS)
        d0 = r0 - r + (WIN_H - 1)
        qrows = pl.ds(pl.multiple_of(r * GRID_W, GRID_W), GRID_W)
        krows = pl.ds(pl.multiple_of(r0 * GRID_W, GRID_W), KEYS_LOCAL)
        q = qb_ref[qrows, :]
        k = jnp.concatenate([kb_ref[krows, :], kc], axis=0)
        v = jnp.concatenate([vb_ref[krows, :], vc], axis=0)
        s = jnp.concatenate([_dot_nt(_pair_queries(q[:, c]), k[:, c]) for c in PAIR_COLS], axis=0)
        bias = jnp.concatenate(
            [jnp.concatenate([tab_ref[h, d0 + 2 * i] for i in range(KEY_ROWS // 2)], axis=-1)
             for h in range(N_HEADS)], axis=0)
        s_loc = jnp.where(col_ok, s[:, :KEYS_LOCAL] + bias, NEG_INF)
        s_ctx = s[:, KEYS_LOCAL:]
        mx = jnp.maximum(jnp.max(s_loc, axis=-1, keepdims=True), jnp.max(s_ctx, axis=-1, keepdims=True))
        e_loc = jnp.exp(s_loc - mx)
        e_ctx = jnp.exp(s_ctx - mx)
        inv = 1.0 / (jnp.sum(e_loc, axis=-1, keepdims=True) + jnp.sum(e_ctx, axis=-1, keepdims=True))
        p = jnp.concatenate([e_loc, e_ctx], axis=-1).astype(BF16)
        outs = []
        for j, c in enumerate(PAIR_COLS):
            pair = slice(2 * j * GRID_W, 2 * (j + 1) * GRID_W)
            outs.append(_pair_outputs(_dot(p[pair, :], v[:, c]) * inv[pair, :]))
        o_ref[qrows, :] = jnp.concatenate(outs, axis=-1).astype(BF16)
        return carry

    lax.fori_loop(0, GRID_ROWS, one_row, 0, unroll=2)


def _natten(proj, cache_k, cache_v, rpb_pad, layer):
    lat0 = CTX_TOKENS // DEC_SEQ
    spec = lambda c: pl.BlockSpec((DEC_SEQ, ATTN_WIDTH), lambda b, c=c: (lat0 + b, c))
    cache_spec = pl.BlockSpec((None, None, PAST_LEN, ATTN_WIDTH), lambda b: (b, layer, 0, 0))
    return pl.pallas_call(
        _natten_kernel,
        out_shape=jax.ShapeDtypeStruct((LAT_TOKENS, ATTN_WIDTH), BF16),
        grid=(DEC_BATCH,),
        in_specs=[spec(0), spec(1), spec(2), cache_spec, cache_spec,
                  pl.BlockSpec((None, N_HEADS, N_DROW, RPB_LANES), lambda b: (layer, 0, 0, 0))],
        out_specs=pl.BlockSpec((DEC_SEQ, ATTN_WIDTH), lambda b: (b, 0)),
        scratch_shapes=[pltpu.VMEM((DEC_SEQ, ATTN_WIDTH), BF16)] * 3
        + [pltpu.VMEM((N_HEADS, N_DROW_PAIRS, GRID_W, RPB_LANES), F32)],
        compiler_params=_params(1),
        name="natten",
    )(proj, proj, proj, cache_k, cache_v, rpb_pad)


def _sgu_kernel(u_ref, v_ref, g_ref, w_ref, b_ref, o_ref):
    u = _gelu_tanh(u_ref[...])
    v = _gelu_tanh(v_ref[...])
    ms = jnp.mean(v * v, axis=-1, keepdims=True)
    vn = v * lax.rsqrt(ms + RMS_EPS) * g_ref[...]
    group = lax.broadcasted_iota(jnp.int32, (CHUNK, SGU_WIDTH), 1) // SGU_GROUP_DIM
    w = w_ref[...].astype(BF16)
    bias = b_ref[...]
    for n in range(TM // CHUNK):
        rows = slice(n * CHUNK, (n + 1) * CHUNK)
        x = vn[rows, :]
        stacked = jnp.concatenate(
            [jnp.where(group == g, x, 0.0) for g in range(SGU_GROUPS)], axis=0).astype(BF16)
        mixed = _dot(w, stacked) + bias
        o_ref[rows, :] = (u[rows, :] * mixed).astype(BF16)


def _sgu(proj, sgu_norm, w_cat, bias_tile, layer):
    return pl.pallas_call(
        _sgu_kernel,
        out_shape=jax.ShapeDtypeStruct((TOKENS, SGU_WIDTH), BF16),
        grid=(N_TILES,),
        in_specs=[
            pl.BlockSpec((TM, SGU_WIDTH), lambda i: (i, US_COL)),
            pl.BlockSpec((TM, SGU_WIDTH), lambda i: (i, VS_COL)),
            pl.BlockSpec((None, 1, SGU_WIDTH), lambda i: (layer, 0, 0)),
            pl.BlockSpec((None, CHUNK, SGU_GROUPS * CHUNK), lambda i: (layer, 0, 0)),
            pl.BlockSpec((None, CHUNK, SGU_WIDTH), lambda i: (layer, 0, 0)),
        ],
        out_specs=pl.BlockSpec((TM, SGU_WIDTH), lambda i: (i, 0)),
        compiler_params=_params(1),
        name="sgu",
    )(proj, proj, sgu_norm, w_cat, bias_tile)


def _dft_tables(n):
    k = np.arange(n, dtype=np.int64)
    ang = 2.0 * np.pi * ((k[:, None] * k[None, :]) % n).astype(np.float64) / n
    return np.cos(ang), np.sin(ang)


def _channel_tables():
    c, s = _dft_tables(FNET_GROUP_DIM)
    eye = np.eye(FNET_GROUPS)
    return np.concatenate([np.kron(eye, c), np.kron(eye, s)], axis=1)


_CHAN_TABLE = _channel_tables().astype(np.float32)
_CTX_POS_TABLE = np.concatenate(_dft_tables(SEQ), axis=1).astype(np.float32)
_LAT_POS_TABLE = np.concatenate(_dft_tables(DEC_SEQ), axis=1).astype(np.float32)
FNET_CTX_SCALE = np.float32(1.0 / np.sqrt(SEQ * FNET_GROUP_DIM))
FNET_LAT_SCALE = np.float32(1.0 / np.sqrt(DEC_SEQ * FNET_GROUP_DIM))


def _fnet_ctx_kernel(f_ref, chan_ref, pos_ref, o_ref):
    y = _dot(f_ref[...].astype(BF16), chan_ref[...].astype(BF16))
    pos = pos_ref[...].astype(BF16)
    for b in range(TM // SEQ):
        rows = slice(b * SEQ, (b + 1) * SEQ)
        yc = y[rows, :FNET_WIDTH].astype(BF16)
        ys = y[rows, FNET_WIDTH:].astype(BF16)
        out = _dot(pos[:, :SEQ], yc) - _dot(pos[:, SEQ:], ys)
        o_ref[rows, :] = (out * FNET_CTX_SCALE).astype(BF16)


def _fnet_ctx(proj, chan, pos):
    whole = lambda i: (0, 0)
    return pl.pallas_call(
        _fnet_ctx_kernel,
        out_shape=jax.ShapeDtypeStruct((CTX_TOKENS, FNET_WIDTH), BF16),
        grid=(CTX_TILES,),
        in_specs=[
            pl.BlockSpec((TM, FNET_WIDTH), lambda i: (i, F_COL)),
            pl.BlockSpec((FNET_WIDTH, 2 * FNET_WIDTH), whole),
            pl.BlockSpec((SEQ, 2 * SEQ), whole),
        ],
        out_specs=pl.BlockSpec((TM, FNET_WIDTH), lambda i: (i, 0)),
        compiler_params=_params(1),
        name="fnet_ctx",
    )(proj, chan, pos)


FNET_LAT_ROWS = 256


def _fnet_lat_kernel(f0_ref, f1_ref, chan_ref, pos_ref, o_ref, yc_ref, ys_ref):
    @pl.when(pl.program_id(0) == 0)
    def _():
        for b, f_ref in enumerate((f0_ref, f1_ref)):
            y = _dot(f_ref[...].astype(BF16), chan_ref[...].astype(BF16))
            cols = slice(b * FNET_WIDTH, (b + 1) * FNET_WIDTH)
            yc_ref[:, cols] = y[:, :FNET_WIDTH].astype(BF16)
            ys_ref[:, cols] = y[:, FNET_WIDTH:].astype(BF16)

    out = (_dot(pos_ref[:, :DEC_SEQ].astype(BF16), yc_ref[...])
           - _dot(pos_ref[:, DEC_SEQ:].astype(BF16), ys_ref[...]))
    out = out * FNET_LAT_SCALE
    for b in range(DEC_BATCH):
        o_ref[b] = out[:, b * FNET_WIDTH:(b + 1) * FNET_WIDTH].astype(BF16)


def _fnet_lat(proj, chan, pos):
    lat0 = CTX_TOKENS // DEC_SEQ
    return pl.pallas_call(
        _fnet_lat_kernel,
        out_shape=jax.ShapeDtypeStruct((DEC_BATCH, DEC_SEQ, FNET_WIDTH), BF16),
        grid=(DEC_SEQ // FNET_LAT_ROWS,),
        in_specs=[
            pl.BlockSpec((DEC_SEQ, FNET_WIDTH), lambda j: (lat0, F_COL)),
            pl.BlockSpec((DEC_SEQ, FNET_WIDTH), lambda j: (lat0 + 1, F_COL)),
            pl.BlockSpec((FNET_WIDTH, 2 * FNET_WIDTH), lambda j: (0, 0)),
            pl.BlockSpec((FNET_LAT_ROWS, 2 * DEC_SEQ), lambda j: (j, 0)),
        ],
        out_specs=pl.BlockSpec((DEC_BATCH, FNET_LAT_ROWS, FNET_WIDTH), lambda j: (0, j, 0)),
        scratch_shapes=[pltpu.VMEM((DEC_SEQ, DEC_BATCH * FNET_WIDTH), BF16)] * 2,
        compiler_params=_params(1),
        name="fnet_lat",
    )(proj, proj, chan, pos)


def _merge_kernel(h_ref, m_ref, g_ref, wg_ref, bg_ref, oac_ref, oal_ref, ob_ref, occ_ref, ocl_ref,
                  pa_ref, pb_ref, pc_ref, wo_ref, o_ref):
    ctx = pl.program_id(0) < CTX_TILES
    x = h_ref[...]
    shift = m_ref[0, 3:4, :]
    scale = m_ref[0, 4:5, :]
    gate = m_ref[0, 5:6, :]
    z = _rms_modulate(x, g_ref[...], shift, scale).astype(BF16)
    oa = jnp.where(ctx, oac_ref[...], oal_ref[...])
    oc = jnp.where(ctx, occ_ref[...], ocl_ref[...])
    branches = ((oa, pa_ref), (ob_ref[...], pb_ref), (oc, pc_ref))
    mix = None
    for j, (o, p_ref) in enumerate(branches):
        cols = slice(j * D_MODEL, (j + 1) * D_MODEL)
        gj = _sigmoid(_dot(z, wg_ref[:, cols]) + bg_ref[j:j + 1, :])
        term = gj * _dot(o, p_ref[...])
        mix = term if mix is None else mix + term
    o_ref[...] = x + gate * _dot(mix.astype(BF16), wo_ref[...])


def _merge(h, mod, gains, w_gates, b_gate, oa_ctx, oa_lat, ob, oc_ctx, oc_lat, p_attn, p_sgu, p_fnet, w_out, layer):
    lw = lambda i: (layer, 0, 0)
    ctx_tile = lambda i: (_ctx_tile(i), 0)
    lat_tile = lambda i: (_lat_tile(i), 0)
    return pl.pallas_call(
        _merge_kernel,
        out_shape=jax.ShapeDtypeStruct((TOKENS, D_MODEL), F32),
        grid=(N_TILES,),
        in_specs=[
            pl.BlockSpec((TM, D_MODEL), lambda i: (i, 0)),
            _mod_spec(layer),
            _gain_spec(layer, 1),
            _single((None, D_MODEL, GATE_COLS), lw),
            pl.BlockSpec((None, N_BRANCH, D_MODEL), lw),
            pl.BlockSpec((TM, ATTN_WIDTH), ctx_tile),
            pl.BlockSpec((TM, ATTN_WIDTH), lat_tile),
            pl.BlockSpec((TM, SGU_WIDTH), lambda i: (i, 0)),
            pl.BlockSpec((TM, FNET_WIDTH), ctx_tile),
            pl.BlockSpec((TM, FNET_WIDTH), lat_tile),
            _single((None, ATTN_WIDTH, D_MODEL), lw),
            _single((None, SGU_WIDTH, D_MODEL), lw),
            _single((None, FNET_WIDTH, D_MODEL), lw),
            _single((None, D_MODEL, D_MODEL), lw),
        ],
        out_specs=pl.BlockSpec((TM, D_MODEL), lambda i: (i, 0)),
        compiler_params=_params(1),
        name="merge",
    )(h, mod, gains, w_gates, b_gate, oa_ctx, oa_lat, ob, oc_ctx, oc_lat, p_attn, p_sgu, p_fnet, w_out)


def kernel(x_prompt, x_sample, cache_k, cache_v, c, c_ctx, w_mod, b_mod, g_norm, ffn_w_gate, ffn_w_up,
           ffn_w_down, w_in, b_gate, rpb, sgu_norm, sgu_w, sgu_b, p_attn, p_sgu, p_fnet, w_out, g_final):
    cond = jnp.concatenate([c_ctx[None, :], c, jnp.zeros((COND_ROWS - N_COND, D_MODEL), F32)], axis=0)
    mod = _modulation(cond, w_mod, b_mod).reshape(DEPTH, COND_ROWS, N_MOD, D_MODEL)
    rpb_pad = jnp.pad(rpb, ((0, 0), (0, 0), (0, 0), (0, RPB_LANES - N_DCOL)))
    cache_k = cache_k.reshape(DEC_BATCH, DEPTH, PAST_LEN, ATTN_WIDTH)
    cache_v = cache_v.reshape(DEC_BATCH, DEPTH, PAST_LEN, ATTN_WIDTH)
    gains = g_norm.reshape(DEPTH, 3, 1, D_MODEL)
    g_final2 = g_final.reshape(1, D_MODEL)

    wg, wu, wd = ffn_w_gate.astype(BF16), ffn_w_up.astype(BF16), ffn_w_down.astype(BF16)
    w_proj = w_in[:, :, :PROJ_COLS].astype(BF16)
    w_gates = w_in[:, :, PROJ_COLS:].astype(BF16)
    pa, pb, pc, wo = p_attn.astype(BF16), p_sgu.astype(BF16), p_fnet.astype(BF16), w_out.astype(BF16)
    sgu_gain = sgu_norm.reshape(DEPTH, 1, SGU_WIDTH)
    w_cat = jnp.transpose(sgu_w, (0, 2, 1, 3)).reshape(DEPTH, CHUNK, SGU_GROUPS * CHUNK)
    bias_tile = jnp.repeat(jnp.transpose(sgu_b, (0, 2, 1)), SGU_GROUP_DIM, axis=2)

    chan = jnp.asarray(_CHAN_TABLE)
    pos_ctx = jnp.asarray(_CTX_POS_TABLE)
    pos_lat = jnp.asarray(_LAT_POS_TABLE)

    kt = jnp.zeros((BATCH, DEPTH, ATTN_WIDTH, SEQ), F32)
    vt = jnp.zeros((BATCH, DEPTH, ATTN_WIDTH, SEQ), F32)

    xs = (x_prompt.reshape(CTX_TOKENS, D_MODEL), x_sample.reshape(LAT_TOKENS, D_MODEL))
    for l in range(DEPTH):
        h = _ffn(xs, mod, gains, wg, wu, wd, g_final2, l, 0, split_in=(l == 0))
        proj, kt, vt = _inproj(h, mod, gains, w_proj, kt, vt, l)
        oa_ctx = _ctx_attention(proj, kt, l)
        oa_lat = _natten(proj, cache_k, cache_v, rpb_pad, l)
        ob = _sgu(proj, sgu_gain, w_cat, bias_tile, l)
        oc_ctx = _fnet_ctx(proj, chan, pos_ctx)
        oc_lat = _fnet_lat(proj, chan, pos_lat).reshape(LAT_TOKENS, FNET_WIDTH)
        h = _merge(h, mod, gains, w_gates, b_gate, oa_ctx, oa_lat, ob, oc_ctx, oc_lat, pa, pb, pc, wo, l)
        xs = _ffn((h,), mod, gains, wg, wu, wd, g_final2, l, 2, split_out=(l == DEPTH - 1))
        xs = xs if l == DEPTH - 1 else (xs,)

    y_prompt, y_sample = xs
    to_cache = lambda t: jnp.transpose(t.reshape(BATCH, DEPTH, N_HEADS, HEAD_DIM, SEQ), (0, 1, 4, 2, 3))
    return (y_prompt.reshape(BATCH, SEQ, D_MODEL), y_sample.reshape(DEC_BATCH, DEC_SEQ, D_MODEL),
            to_cache(kt), to_cache(vt))
```

```python
import functools

import numpy as np
import jax
import jax.numpy as jnp
from jax import lax
from jax.experimental import pallas as pl
from jax.experimental.pallas import tpu as pltpu

D_MODEL = 1024
BATCH = 32
SEQ = 256
DEPTH = 4
DEC_BATCH = 2
DEC_SEQ = 2048
PAST_LEN = 256
GRID_W = 64
GRID_ROWS = DEC_SEQ // GRID_W
WIN_H = 8
WIN_W = 16
HEAD_DIM = 64
ATTN_WIDTH = 512
N_HEADS = 8
SGU_WIDTH = 256
SGU_GROUPS = 4
SGU_GROUP_DIM = 64
CHUNK = 128
FNET_WIDTH = 256
FNET_GROUPS = 4
FNET_GROUP_DIM = 64
N_BRANCH = 3
D_FF = 2816
N_MOD = 9
RMS_EPS = 1e-6
NEG_INF = -1e30

CTX_TOKENS = BATCH * SEQ
LAT_TOKENS = DEC_BATCH * DEC_SEQ
TOKENS = CTX_TOKENS + LAT_TOKENS
N_COND = 1 + DEC_BATCH
COND_ROWS = 8
PROJ_COLS = 3 * ATTN_WIDTH + 2 * SGU_WIDTH + FNET_WIDTH
GATE_COLS = N_BRANCH * D_MODEL
COL_BLOCK = 256
US_COL, VS_COL, F_COL = 6, 7, 8

TM = 1024
N_TILES = TOKENS // TM
CTX_TILES = CTX_TOKENS // TM
TILES_PER_LAT = DEC_SEQ // TM
REQ_PER_TILE = TM // SEQ
FF_CHUNK = 256
VMEM_LIMIT = 56 * 1024 * 1024

BF16 = jnp.bfloat16
F32 = jnp.float32
ATTN_SCALE = np.float32(HEAD_DIM ** -0.5)


def _cond_row(i):
    return jnp.where(i < CTX_TILES, 0, 1 + (i - CTX_TILES) // TILES_PER_LAT)


def _ctx_tile(i):
    return jnp.minimum(i, CTX_TILES - 1)


def _lat_tile(i):
    return jnp.maximum(i - CTX_TILES, 0)


def _params(n_axes):
    return pltpu.CompilerParams(dimension_semantics=("arbitrary",) * n_axes,
                                vmem_limit_bytes=VMEM_LIMIT)


def _dot(a, b):
    return jnp.dot(a, b, preferred_element_type=F32)


def _dot_nt(a, b):
    return lax.dot_general(a, b, (((1,), (1,)), ((), ())), preferred_element_type=F32)


def _rms_modulate(x, g, shift, scale):
    ms = jnp.mean(x * x, axis=-1, keepdims=True)
    y = x * lax.rsqrt(ms + RMS_EPS) * g
    return y * (1.0 + scale) + shift


def _silu(x):
    return x * (1.0 / (1.0 + jnp.exp(-x)))


def _sigmoid(x):
    return 1.0 / (1.0 + jnp.exp(-x))


def _gelu_tanh(x):
    c = np.float32(np.sqrt(2.0 / np.pi))
    return x * (0.5 * (1.0 + jnp.tanh(c * (x + np.float32(0.044715) * (x * x * x)))))


MOD_COLS = N_MOD * D_MODEL
MOD_BLOCK = 1152


def _mod_kernel(c_ref, w_ref, b_ref, o_ref):
    a = _silu(c_ref[...]).astype(BF16)
    o_ref[0] = _dot(a, w_ref[0].astype(BF16)) + b_ref[0]


def _modulation(cond, w_mod, b_mod):
    return pl.pallas_call(
        _mod_kernel,
        out_shape=jax.ShapeDtypeStruct((DEPTH, COND_ROWS, MOD_COLS), F32),
        grid=(DEPTH, MOD_COLS // MOD_BLOCK),
        in_specs=[
            pl.BlockSpec((COND_ROWS, D_MODEL), lambda l, j: (0, 0)),
            pl.BlockSpec((1, D_MODEL, MOD_BLOCK), lambda l, j: (l, 0, j)),
            pl.BlockSpec((1, 1, MOD_BLOCK), lambda l, j: (l, 0, j)),
        ],
        out_specs=pl.BlockSpec((1, COND_ROWS, MOD_BLOCK), lambda l, j: (l, 0, j)),
        compiler_params=_params(2),
        name="modulation",
    )(cond, w_mod, b_mod.reshape(DEPTH, 1, MOD_COLS))


def _mod_spec(layer):
    return pl.BlockSpec((None, 1, N_MOD, D_MODEL), lambda i: (layer, _cond_row(i), 0, 0))


def _gain_spec(layer, sub):
    return pl.BlockSpec((None, None, 1, D_MODEL), lambda i: (layer, sub, 0, 0))


STAGE_ROWS = 256
STAGE_SLOTS = 2


def _stage_weights(jobs, stage_ref, sem_ref):
    def copy(k):
        src = jobs[k][0]
        rows, cols = src.shape
        return pltpu.make_async_copy(src, stage_ref.at[k % STAGE_SLOTS, :rows, :cols], sem_ref.at[k % STAGE_SLOTS])

    copy(0).start()
    for k, (src, dst) in enumerate(jobs):
        if k + 1 < len(jobs):
            copy(k + 1).start()
        copy(k).wait()
        rows, cols = src.shape
        dst[...] = stage_ref[k % STAGE_SLOTS, :rows, :cols].astype(BF16)


def _row_chunks(src, dst, rows):
    return [(src.at[pl.ds(r, STAGE_ROWS), :], dst.at[pl.ds(r, STAGE_ROWS), :]) for r in range(0, rows, STAGE_ROWS)]


N_FF_CHUNKS = D_FF // FF_CHUNK
FFN_WEIGHTS = 3


def _ffn_kernel(*refs, mod_base, split_in, split_out, layer, ffn_sub):
    n_in = 2 if split_in else 1
    x_refs, (m_ref, g_ref, wg_hbm, wu_hbm, wd_hbm, gf_ref) = refs[:n_in], refs[n_in:n_in + 6]
    n_out = 2 if split_out else 1
    o_refs = refs[n_in + 6:n_in + 6 + n_out]
    z_ref, acc_ref, wg_ref, wu_ref, wd_ref, stage_ref, stage_d_ref, sem_ref = refs[n_in + 6 + n_out:]
    first_tile = pl.program_id(0) == 0
    is_ctx = pl.program_id(0) < CTX_TILES

    x = jnp.where(is_ctx, x_refs[0][...], x_refs[1][...]) if split_in else x_refs[0][...]
    shift = m_ref[0, mod_base:mod_base + 1, :]
    scale = m_ref[0, mod_base + 1:mod_base + 2, :]
    gate = m_ref[0, mod_base + 2:mod_base + 3, :]
    z_ref[...] = _rms_modulate(x, g_ref[...], shift, scale).astype(BF16)

    def chunk(j):
        z = z_ref[...]
        act = (_silu(_dot(z, wg_ref[j])) * _dot(z, wu_ref[j])).astype(BF16)
        return _dot(act, wd_ref[j])

    def copies(j, slot):
        cols = pl.ds(pl.multiple_of(j * FF_CHUNK, FF_CHUNK), FF_CHUNK)
        srcs = (wg_hbm.at[layer, ffn_sub, :, cols], wu_hbm.at[layer, ffn_sub, :, cols], wd_hbm.at[layer, ffn_sub, cols, :])
        dsts = (stage_ref.at[0, slot], stage_ref.at[1, slot], stage_d_ref.at[slot])
        return [pltpu.make_async_copy(src, dst, sem_ref.at[w, slot]) for w, (src, dst) in enumerate(zip(srcs, dsts))]

    @pl.when(first_tile)
    def _():
        for j in range(STAGE_SLOTS):
            for c in copies(j, j):
                c.start()
        acc_ref[...] = jnp.zeros_like(acc_ref)

        def staged_chunk(j, carry):
            slot = j % STAGE_SLOTS
            for c in copies(j, slot):
                c.wait()
            wg_ref[j] = stage_ref[0, slot].astype(BF16)
            wu_ref[j] = stage_ref[1, slot].astype(BF16)
            wd_ref[j] = stage_d_ref[slot].astype(BF16)

            @pl.when(j + STAGE_SLOTS < N_FF_CHUNKS)
            def _():
                for c in copies(j + STAGE_SLOTS, slot):
                    c.start()

            acc_ref[...] += chunk(j)
            return carry

        lax.fori_loop(0, N_FF_CHUNKS, staged_chunk, 0)

    @pl.when(jnp.logical_not(first_tile))
    def _():
        for j in range(N_FF_CHUNKS):
            if j == 0:
                acc_ref[...] = chunk(j)
            else:
                acc_ref[...] += chunk(j)

    y = x + (0.5 * gate) * acc_ref[...]
    if not split_out:
        o_refs[0][...] = y
    else:
        ms = jnp.mean(y * y, axis=-1, keepdims=True)
        y = y * lax.rsqrt(ms + RMS_EPS) * gf_ref[...]

        @pl.when(is_ctx)
        def _():
            o_refs[0][...] = y

        @pl.when(jnp.logical_not(is_ctx))
        def _():
            o_refs[1][...] = y


def _ffn(xs, mod, gains, wg, wu, wd, g_final, layer, sub, *, split_in=False, split_out=False):
    kern = functools.partial(_ffn_kernel, mod_base=0 if sub == 0 else 6, split_in=split_in, split_out=split_out,
                             layer=layer, ffn_sub=0 if sub == 0 else 1)
    row = pl.BlockSpec((TM, D_MODEL), lambda i: (i, 0))
    split_rows = [pl.BlockSpec((TM, D_MODEL), lambda i: (_ctx_tile(i), 0), pipeline_mode=pl.Buffered(1)),
                  pl.BlockSpec((TM, D_MODEL), lambda i: (_lat_tile(i), 0), pipeline_mode=pl.Buffered(1))]
    hbm = pl.BlockSpec(memory_space=pl.ANY)
    if split_out:
        out_shape = (jax.ShapeDtypeStruct((CTX_TOKENS, D_MODEL), F32), jax.ShapeDtypeStruct((LAT_TOKENS, D_MODEL), F32))
        out_specs = tuple(split_rows)
    else:
        out_shape = jax.ShapeDtypeStruct((TOKENS, D_MODEL), F32)
        out_specs = row
    return pl.pallas_call(
        kern,
        out_shape=out_shape,
        grid=(N_TILES,),
        in_specs=(split_rows if split_in else [row]) + [
            _mod_spec(layer),
            _gain_spec(layer, sub),
            hbm, hbm, hbm,
            pl.BlockSpec((1, D_MODEL), lambda i: (0, 0)),
        ],
        out_specs=out_specs,
        scratch_shapes=[
            pltpu.VMEM((TM, D_MODEL), BF16),
            pltpu.VMEM((TM, D_MODEL), F32),
            pltpu.VMEM((N_FF_CHUNKS, D_MODEL, FF_CHUNK), BF16),
            pltpu.VMEM((N_FF_CHUNKS, D_MODEL, FF_CHUNK), BF16),
            pltpu.VMEM((N_FF_CHUNKS, FF_CHUNK, D_MODEL), BF16),
            pltpu.VMEM((2, STAGE_SLOTS, D_MODEL, FF_CHUNK), F32),
            pltpu.VMEM((STAGE_SLOTS, FF_CHUNK, D_MODEL), F32),
            pltpu.SemaphoreType.DMA((FFN_WEIGHTS, STAGE_SLOTS)),
        ],
        compiler_params=_params(1),
        name="ffn",
    )(*xs, mod, gains, wg, wu, wd, g_final)


def _inproj_kernel(h_ref, m_ref, g_ref, w_hbm, kt_in, vt_in, p_ref, kt_ref, vt_ref, w_ref, stage_ref, sem_ref, *, layer):
    del kt_in, vt_in

    @pl.when(pl.program_id(0) == 0)
    def _():
        _stage_weights(_row_chunks(w_hbm.at[layer, :, pl.ds(0, PROJ_COLS)], w_ref, D_MODEL), stage_ref, sem_ref)

    shift = m_ref[0, 3:4, :]
    scale = m_ref[0, 4:5, :]
    z = _rms_modulate(h_ref[...], g_ref[...], shift, scale).astype(BF16)
    p_ref[...] = _dot(z, w_ref[...])

    @pl.when(pl.program_id(0) < CTX_TILES)
    def _():
        for b in range(REQ_PER_TILE):
            rows = slice(b * SEQ, (b + 1) * SEQ)
            kt_ref[b] = p_ref[rows, ATTN_WIDTH:2 * ATTN_WIDTH].T
            vt_ref[b] = p_ref[rows, 2 * ATTN_WIDTH:3 * ATTN_WIDTH].T


def _inproj(h, mod, gains, w_in, kt, vt, layer):
    cache_shape = jax.ShapeDtypeStruct((BATCH, DEPTH, ATTN_WIDTH, SEQ), F32)
    cache_spec = pl.BlockSpec((REQ_PER_TILE, None, ATTN_WIDTH, SEQ), lambda i: (_ctx_tile(i), layer, 0, 0))
    return pl.pallas_call(
        functools.partial(_inproj_kernel, layer=layer),
        out_shape=(jax.ShapeDtypeStruct((TOKENS, PROJ_COLS), F32), cache_shape, cache_shape),
        grid=(N_TILES,),
        in_specs=[
            pl.BlockSpec((TM, D_MODEL), lambda i: (i, 0)),
            _mod_spec(layer),
            _gain_spec(layer, 1),
            pl.BlockSpec(memory_space=pl.ANY),
            pl.BlockSpec(memory_space=pl.ANY),
            pl.BlockSpec(memory_space=pl.ANY),
        ],
        out_specs=(pl.BlockSpec((TM, PROJ_COLS), lambda i: (i, 0)), cache_spec, cache_spec),
        input_output_aliases={4: 1, 5: 2},
        scratch_shapes=[
            pltpu.VMEM((D_MODEL, PROJ_COLS), BF16),
            pltpu.VMEM((STAGE_SLOTS, STAGE_ROWS, PROJ_COLS), F32),
            pltpu.SemaphoreType.DMA((STAGE_SLOTS,)),
        ],
        compiler_params=_params(1),
        name="inproj",
    )(h, mod, gains, w_in, kt, vt)


HEAD_PAIRS = N_HEADS // 2
PAIR_WIDTH = 2 * HEAD_DIM
PAIR_COLS = [slice(j * PAIR_WIDTH, (j + 1) * PAIR_WIDTH) for j in range(HEAD_PAIRS)]


def _first_of_pair(rows):
    return lax.broadcasted_iota(jnp.int32, (rows, PAIR_WIDTH), 1) < HEAD_DIM


def _pair_queries(q):
    first = _first_of_pair(q.shape[0])
    zero = jnp.zeros_like(q)
    return jnp.concatenate([jnp.where(first, q, zero), jnp.where(first, zero, q)], axis=0)


def _pair_outputs(o):
    m = o.shape[0] // 2
    return jnp.where(_first_of_pair(m), o[:m], o[m:])


def _ctx_attn_kernel(q_ref, kt_ref, v_ref, o_ref):
    def one_request(b, carry):
        rows = pl.ds(pl.multiple_of(b * SEQ, SEQ), SEQ)
        q = (q_ref[rows, :] * ATTN_SCALE).astype(BF16)
        kt = kt_ref[b].astype(BF16)
        v = v_ref[rows, :].astype(BF16)
        s = jnp.concatenate([_dot(_pair_queries(q[:, c]), kt[c, :]) for c in PAIR_COLS], axis=0)
        e = jnp.exp(s - jnp.max(s, axis=-1, keepdims=True))
        inv = 1.0 / jnp.sum(e, axis=-1, keepdims=True)
        p = e.astype(BF16)
        outs = []
        for j, c in enumerate(PAIR_COLS):
            pair = slice(2 * j * SEQ, 2 * (j + 1) * SEQ)
            outs.append(_pair_outputs(_dot(p[pair, :], v[:, c]) * inv[pair, :]))
        o_ref[rows, :] = jnp.concatenate(outs, axis=-1).astype(BF16)
        return carry

    lax.fori_loop(0, REQ_PER_TILE, one_request, 0, unroll=2)


def _ctx_attention(proj, kt, layer):
    return pl.pallas_call(
        _ctx_attn_kernel,
        out_shape=jax.ShapeDtypeStruct((CTX_TOKENS, ATTN_WIDTH), BF16),
        grid=(CTX_TILES,),
        in_specs=[
            pl.BlockSpec((TM, ATTN_WIDTH), lambda i: (i, 0)),
            pl.BlockSpec((REQ_PER_TILE, None, ATTN_WIDTH, SEQ), lambda i: (i, layer, 0, 0)),
            pl.BlockSpec((TM, ATTN_WIDTH), lambda i: (i, 2)),
        ],
        out_specs=pl.BlockSpec((TM, ATTN_WIDTH), lambda i: (i, 0)),
        compiler_params=_params(1),
        name="ctx_attention",
    )(proj, kt, proj)


N_DROW = 2 * WIN_H - 1
N_DCOL = 2 * WIN_W - 1
N_DROW_PAIRS = N_DROW - 1
KEY_ROWS = WIN_H
KEYS_LOCAL = KEY_ROWS * GRID_W
RPB_LANES = 2 * GRID_W


def _build_bias_table(rpb_ref, tab_ref):
    first = _first_of_pair(GRID_W)
    for h in range(N_HEADS):
        for d in range(N_DROW_PAIRS):
            lo = jnp.broadcast_to(rpb_ref[h, d:d + 1, :], (GRID_W, RPB_LANES))
            hi = jnp.broadcast_to(rpb_ref[h, d + 1:d + 2, :], (GRID_W, RPB_LANES))
            lo = pltpu.roll(lo, RPB_LANES - (WIN_W - 1), 1, stride=1, stride_axis=0)
            hi = pltpu.roll(hi, GRID_W - (WIN_W - 1), 1, stride=1, stride_axis=0)
            tab_ref[h, d] = jnp.where(first, lo, hi)


def _natten_kernel(q_ref, k_ref, v_ref, kc_ref, vc_ref, rpb_ref, o_ref, qb_ref, kb_ref, vb_ref, tab_ref):
    @pl.when(pl.program_id(0) == 0)
    def _():
        _build_bias_table(rpb_ref, tab_ref)

    qb_ref[...] = (q_ref[...] * ATTN_SCALE).astype(BF16)
    kb_ref[...] = k_ref[...].astype(BF16)
    vb_ref[...] = v_ref[...].astype(BF16)
    kc = kc_ref[...].astype(BF16)
    vc = vc_ref[...].astype(BF16)
    qcol = lax.broadcasted_iota(jnp.int32, (N_HEADS * GRID_W, KEYS_LOCAL), 0) % GRID_W
    kcol = lax.broadcasted_iota(jnp.int32, (N_HEADS * GRID_W, KEYS_LOCAL), 1) % GRID_W
    start = jnp.clip(qcol - WIN_W // 2, 0, GRID_W - WIN_W)
    col_ok = (kcol >= start) & (kcol < start + WIN_W)

    def one_row(r, carry):
        r0 = jnp.clip(r - KEY_ROWS // 2, 0, GRID_ROWS - KEY_ROWS)
        d0 = r0 - r + (WIN_H - 1)
        qrows = pl.ds(pl.multiple_of(r * GRID_W, GRID_W), GRID_W)
        krows = pl.ds(pl.multiple_of(r0 * GRID_W, GRID_W), KEYS_LOCAL)
        q = qb_ref[qrows, :]
        k = jnp.concatenate([kb_ref[krows, :], kc], axis=0)
        v = jnp.concatenate([vb_ref[krows, :], vc], axis=0)
        s = jnp.concatenate([_dot_nt(_pair_queries(q[:, c]), k[:, c]) for c in PAIR_COLS], axis=0)
        bias = jnp.concatenate(
            [jnp.concatenate([tab_ref[h, d0 + 2 * i] for i in range(KEY_ROWS // 2)], axis=-1)
             for h in range(N_HEADS)], axis=0)
        s_loc = jnp.where(col_ok, s[:, :KEYS_LOCAL] + bias, NEG_INF)
        s_ctx = s[:, KEYS_LOCAL:]
        mx = jnp.maximum(jnp.max(s_loc, axis=-1, keepdims=True), jnp.max(s_ctx, axis=-1, keepdims=True))
        e_loc = jnp.exp(s_loc - mx)
        e_ctx = jnp.exp(s_ctx - mx)
        inv = 1.0 / (jnp.sum(e_loc, axis=-1, keepdims=True) + jnp.sum(e_ctx, axis=-1, keepdims=True))
        p = jnp.concatenate([e_loc, e_ctx], axis=-1).astype(BF16)
        outs = []
        for j, c in enumerate(PAIR_COLS):
            pair = slice(2 * j * GRID_W, 2 * (j + 1) * GRID_W)
            outs.append(_pair_outputs(_dot(p[pair, :], v[:, c]) * inv[pair, :]))
        o_ref[qrows, :] = jnp.concatenate(outs, axis=-1).astype(BF16)
        return carry

    lax.fori_loop(0, GRID_ROWS, one_row, 0, unroll=2)


def _natten(proj, cache_k, cache_v, rpb_pad, layer):
    lat0 = CTX_TOKENS // DEC_SEQ
    spec = lambda c: pl.BlockSpec((DEC_SEQ, ATTN_WIDTH), lambda b, c=c: (lat0 + b, c))
    cache_spec = pl.BlockSpec((None, None, PAST_LEN, ATTN_WIDTH), lambda b: (b, layer, 0, 0))
    return pl.pallas_call(
        _natten_kernel,
        out_shape=jax.ShapeDtypeStruct((LAT_TOKENS, ATTN_WIDTH), BF16),
        grid=(DEC_BATCH,),
        in_specs=[spec(0), spec(1), spec(2), cache_spec, cache_spec,
                  pl.BlockSpec((None, N_HEADS, N_DROW, RPB_LANES), lambda b: (layer, 0, 0, 0))],
        out_specs=pl.BlockSpec((DEC_SEQ, ATTN_WIDTH), lambda b: (b, 0)),
        scratch_shapes=[pltpu.VMEM((DEC_SEQ, ATTN_WIDTH), BF16)] * 3
        + [pltpu.VMEM((N_HEADS, N_DROW_PAIRS, GRID_W, RPB_LANES), F32)],
        compiler_params=_params(1),
        name="natten",
    )(proj, proj, proj, cache_k, cache_v, rpb_pad)


def _sgu_kernel(u_ref, v_ref, g_ref, w_ref, b_ref, o_ref):
    u = _gelu_tanh(u_ref[...])
    v = _gelu_tanh(v_ref[...])
    ms = jnp.mean(v * v, axis=-1, keepdims=True)
    vn = v * lax.rsqrt(ms + RMS_EPS) * g_ref[...]
    group = lax.broadcasted_iota(jnp.int32, (CHUNK, SGU_WIDTH), 1) // SGU_GROUP_DIM
    w = w_ref[...].astype(BF16)
    bias = b_ref[...]
    for n in range(TM // CHUNK):
        rows = slice(n * CHUNK, (n + 1) * CHUNK)
        x = vn[rows, :]
        stacked = jnp.concatenate(
            [jnp.where(group == g, x, 0.0) for g in range(SGU_GROUPS)], axis=0).astype(BF16)
        mixed = _dot(w, stacked) + bias
        o_ref[rows, :] = (u[rows, :] * mixed).astype(BF16)


def _sgu(proj, sgu_norm, w_cat, bias_tile, layer):
    return pl.pallas_call(
        _sgu_kernel,
        out_shape=jax.ShapeDtypeStruct((TOKENS, SGU_WIDTH), BF16),
        grid=(N_TILES,),
        in_specs=[
            pl.BlockSpec((TM, SGU_WIDTH), lambda i: (i, US_COL)),
            pl.BlockSpec((TM, SGU_WIDTH), lambda i: (i, VS_COL)),
            pl.BlockSpec((None, 1, SGU_WIDTH), lambda i: (layer, 0, 0)),
            pl.BlockSpec((None, CHUNK, SGU_GROUPS * CHUNK), lambda i: (layer, 0, 0)),
            pl.BlockSpec((None, CHUNK, SGU_WIDTH), lambda i: (layer, 0, 0)),
        ],
        out_specs=pl.BlockSpec((TM, SGU_WIDTH), lambda i: (i, 0)),
        compiler_params=_params(1),
        name="sgu",
    )(proj, proj, sgu_norm, w_cat, bias_tile)


def _dft_tables(n):
    k = np.arange(n, dtype=np.int64)
    ang = 2.0 * np.pi * ((k[:, None] * k[None, :]) % n).astype(np.float64) / n
    return np.cos(ang), np.sin(ang)


def _channel_tables():
    c, s = _dft_tables(FNET_GROUP_DIM)
    eye = np.eye(FNET_GROUPS)
    return np.concatenate([np.kron(eye, c), np.kron(eye, s)], axis=1)


_CHAN_TABLE = _channel_tables().astype(np.float32)
_CTX_POS_TABLE = np.concatenate(_dft_tables(SEQ), axis=1).astype(np.float32)
_LAT_POS_TABLE = np.concatenate(_dft_tables(DEC_SEQ), axis=1).astype(np.float32)
FNET_CTX_SCALE = np.float32(1.0 / np.sqrt(SEQ * FNET_GROUP_DIM))
FNET_LAT_SCALE = np.float32(1.0 / np.sqrt(DEC_SEQ * FNET_GROUP_DIM))


def _fnet_ctx_kernel(f_ref, chan_ref, pos_ref, o_ref):
    y = _dot(f_ref[...].astype(BF16), chan_ref[...].astype(BF16))
    pos = pos_ref[...].astype(BF16)
    for b in range(TM // SEQ):
        rows = slice(b * SEQ, (b + 1) * SEQ)
        yc = y[rows, :FNET_WIDTH].astype(BF16)
        ys = y[rows, FNET_WIDTH:].astype(BF16)
        out = _dot(pos[:, :SEQ], yc) - _dot(pos[:, SEQ:], ys)
        o_ref[rows, :] = (out * FNET_CTX_SCALE).astype(BF16)


def _fnet_ctx(proj, chan, pos):
    whole = lambda i: (0, 0)
    return pl.pallas_call(
        _fnet_ctx_kernel,
        out_shape=jax.ShapeDtypeStruct((CTX_TOKENS, FNET_WIDTH), BF16),
        grid=(CTX_TILES,),
        in_specs=[
            pl.BlockSpec((TM, FNET_WIDTH), lambda i: (i, F_COL)),
            pl.BlockSpec((FNET_WIDTH, 2 * FNET_WIDTH), whole),
            pl.BlockSpec((SEQ, 2 * SEQ), whole),
        ],
        out_specs=pl.BlockSpec((TM, FNET_WIDTH), lambda i: (i, 0)),
        compiler_params=_params(1),
        name="fnet_ctx",
    )(proj, chan, pos)


FNET_LAT_ROWS = 256


def _fnet_lat_kernel(f0_ref, f1_ref, chan_ref, pos_ref, o_ref, yc_ref, ys_ref):
    @pl.when(pl.program_id(0) == 0)
    def _():
        for b, f_ref in enumerate((f0_ref, f1_ref)):
            y = _dot(f_ref[...].astype(BF16), chan_ref[...].astype(BF16))
            cols = slice(b * FNET_WIDTH, (b + 1) * FNET_WIDTH)
            yc_ref[:, cols] = y[:, :FNET_WIDTH].astype(BF16)
            ys_ref[:, cols] = y[:, FNET_WIDTH:].astype(BF16)

    out = (_dot(pos_ref[:, :DEC_SEQ].astype(BF16), yc_ref[...])
           - _dot(pos_ref[:, DEC_SEQ:].astype(BF16), ys_ref[...]))
    out = out * FNET_LAT_SCALE
    for b in range(DEC_BATCH):
        o_ref[b] = out[:, b * FNET_WIDTH:(b + 1) * FNET_WIDTH].astype(BF16)


def _fnet_lat(proj, chan, pos):
    lat0 = CTX_TOKENS // DEC_SEQ
    return pl.pallas_call(
        _fnet_lat_kernel,
        out_shape=jax.ShapeDtypeStruct((DEC_BATCH, DEC_SEQ, FNET_WIDTH), BF16),
        grid=(DEC_SEQ // FNET_LAT_ROWS,),
        in_specs=[
            pl.BlockSpec((DEC_SEQ, FNET_WIDTH), lambda j: (lat0, F_COL)),
            pl.BlockSpec((DEC_SEQ, FNET_WIDTH), lambda j: (lat0 + 1, F_COL)),
            pl.BlockSpec((FNET_WIDTH, 2 * FNET_WIDTH), lambda j: (0, 0)),
            pl.BlockSpec((FNET_LAT_ROWS, 2 * DEC_SEQ), lambda j: (j, 0)),
        ],
        out_specs=pl.BlockSpec((DEC_BATCH, FNET_LAT_ROWS, FNET_WIDTH), lambda j: (0, j, 0)),
        scratch_shapes=[pltpu.VMEM((DEC_SEQ, DEC_BATCH * FNET_WIDTH), BF16)] * 2,
        compiler_params=_params(1),
        name="fnet_lat",
    )(proj, proj, chan, pos)


def _merge_kernel(h_ref, m_ref, g_ref, w_in_hbm, bg_ref, oac_ref, oal_ref, ob_ref, occ_ref, ocl_ref,
                  pa_hbm, pb_hbm, pc_hbm, wo_hbm, o_ref, wg_ref, pa_ref, pb_ref, pc_ref, wo_ref, stage_ref, sem_ref,
                  *, layer):
    @pl.when(pl.program_id(0) == 0)
    def _():
        jobs = (_row_chunks(w_in_hbm.at[layer, :, pl.ds(PROJ_COLS, GATE_COLS)], wg_ref, D_MODEL)
                + _row_chunks(pa_hbm.at[layer], pa_ref, ATTN_WIDTH)
                + _row_chunks(pb_hbm.at[layer], pb_ref, SGU_WIDTH)
                + _row_chunks(pc_hbm.at[layer], pc_ref, FNET_WIDTH)
                + _row_chunks(wo_hbm.at[layer], wo_ref, D_MODEL))
        _stage_weights(jobs, stage_ref, sem_ref)

    ctx = pl.program_id(0) < CTX_TILES
    x = h_ref[...]
    shift = m_ref[0, 3:4, :]
    scale = m_ref[0, 4:5, :]
    gate = m_ref[0, 5:6, :]
    z = _rms_modulate(x, g_ref[...], shift, scale).astype(BF16)
    oa = jnp.where(ctx, oac_ref[...], oal_ref[...])
    oc = jnp.where(ctx, occ_ref[...], ocl_ref[...])
    branches = ((oa, pa_ref), (ob_ref[...], pb_ref), (oc, pc_ref))
    mix = None
    for j, (o, p_ref) in enumerate(branches):
        cols = slice(j * D_MODEL, (j + 1) * D_MODEL)
        gj = _sigmoid(_dot(z, wg_ref[:, cols]) + bg_ref[j:j + 1, :])
        term = gj * _dot(o, p_ref[...])
        mix = term if mix is None else mix + term
    o_ref[...] = x + gate * _dot(mix.astype(BF16), wo_ref[...])


def _merge(h, mod, gains, w_in, b_gate, oa_ctx, oa_lat, ob, oc_ctx, oc_lat, p_attn, p_sgu, p_fnet, w_out, layer):
    lw = lambda i: (layer, 0, 0)
    ctx_tile = lambda i: (_ctx_tile(i), 0)
    lat_tile = lambda i: (_lat_tile(i), 0)
    hbm = pl.BlockSpec(memory_space=pl.ANY)
    return pl.pallas_call(
        functools.partial(_merge_kernel, layer=layer),
        out_shape=jax.ShapeDtypeStruct((TOKENS, D_MODEL), F32),
        grid=(N_TILES,),
        in_specs=[
            pl.BlockSpec((TM, D_MODEL), lambda i: (i, 0)),
            _mod_spec(layer),
            _gain_spec(layer, 1),
            hbm,
            pl.BlockSpec((None, N_BRANCH, D_MODEL), lw),
            pl.BlockSpec((TM, ATTN_WIDTH), ctx_tile),
            pl.BlockSpec((TM, ATTN_WIDTH), lat_tile),
            pl.BlockSpec((TM, SGU_WIDTH), lambda i: (i, 0)),
            pl.BlockSpec((TM, FNET_WIDTH), ctx_tile),
            pl.BlockSpec((TM, FNET_WIDTH), lat_tile),
            hbm, hbm, hbm, hbm,
        ],
        out_specs=pl.BlockSpec((TM, D_MODEL), lambda i: (i, 0)),
        scratch_shapes=[
            pltpu.VMEM((D_MODEL, GATE_COLS), BF16),
            pltpu.VMEM((ATTN_WIDTH, D_MODEL), BF16),
            pltpu.VMEM((SGU_WIDTH, D_MODEL), BF16),
            pltpu.VMEM((FNET_WIDTH, D_MODEL), BF16),
            pltpu.VMEM((D_MODEL, D_MODEL), BF16),
            pltpu.VMEM((STAGE_SLOTS, STAGE_ROWS, GATE_COLS), F32),
            pltpu.SemaphoreType.DMA((STAGE_SLOTS,)),
        ],
        compiler_params=_params(1),
        name="merge",
    )(h, mod, gains, w_in, b_gate, oa_ctx, oa_lat, ob, oc_ctx, oc_lat, p_attn, p_sgu, p_fnet, w_out)


def kernel(x_prompt, x_sample, cache_k, cache_v, c, c_ctx, w_mod, b_mod, g_norm, ffn_w_gate, ffn_w_up,
           ffn_w_down, w_in, b_gate, rpb, sgu_norm, sgu_w, sgu_b, p_attn, p_sgu, p_fnet, w_out, g_final):
    cond = jnp.concatenate([c_ctx[None, :], c, jnp.zeros((COND_ROWS - N_COND, D_MODEL), F32)], axis=0)
    mod = _modulation(cond, w_mod, b_mod).reshape(DEPTH, COND_ROWS, N_MOD, D_MODEL)
    rpb_pad = jnp.pad(rpb, ((0, 0), (0, 0), (0, 0), (0, RPB_LANES - N_DCOL)))
    cache_k = cache_k.reshape(DEC_BATCH, DEPTH, PAST_LEN, ATTN_WIDTH)
    cache_v = cache_v.reshape(DEC_BATCH, DEPTH, PAST_LEN, ATTN_WIDTH)
    gains = g_norm.reshape(DEPTH, 3, 1, D_MODEL)
    g_final2 = g_final.reshape(1, D_MODEL)

    sgu_gain = sgu_norm.reshape(DEPTH, 1, SGU_WIDTH)
    w_cat = jnp.transpose(sgu_w, (0, 2, 1, 3)).reshape(DEPTH, CHUNK, SGU_GROUPS * CHUNK)
    bias_tile = jnp.repeat(jnp.transpose(sgu_b, (0, 2, 1)), SGU_GROUP_DIM, axis=2)

    chan = jnp.asarray(_CHAN_TABLE)
    pos_ctx = jnp.asarray(_CTX_POS_TABLE)
    pos_lat = jnp.asarray(_LAT_POS_TABLE)

    kt = jnp.zeros((BATCH, DEPTH, ATTN_WIDTH, SEQ), F32)
    vt = jnp.zeros((BATCH, DEPTH, ATTN_WIDTH, SEQ), F32)

    xs = (x_prompt.reshape(CTX_TOKENS, D_MODEL), x_sample.reshape(LAT_TOKENS, D_MODEL))
    for l in range(DEPTH):
        h = _ffn(xs, mod, gains, ffn_w_gate, ffn_w_up, ffn_w_down, g_final2, l, 0, split_in=(l == 0))
        proj, kt, vt = _inproj(h, mod, gains, w_in, kt, vt, l)
        oa_ctx = _ctx_attention(proj, kt, l)
        oa_lat = _natten(proj, cache_k, cache_v, rpb_pad, l)
        ob = _sgu(proj, sgu_gain, w_cat, bias_tile, l)
        oc_ctx = _fnet_ctx(proj, chan, pos_ctx)
        oc_lat = _fnet_lat(proj, chan, pos_lat).reshape(LAT_TOKENS, FNET_WIDTH)
        h = _merge(h, mod, gains, w_in, b_gate, oa_ctx, oa_lat, ob, oc_ctx, oc_lat, p_attn, p_sgu, p_fnet, w_out, l)
        xs = _ffn((h,), mod, gains, ffn_w_gate, ffn_w_up, ffn_w_down, g_final2, l, 2, split_out=(l == DEPTH - 1))
        xs = xs if l == DEPTH - 1 else (xs,)

    y_prompt, y_sample = xs
    to_cache = lambda t: jnp.transpose(t.reshape(BATCH, DEPTH, N_HEADS, HEAD_DIM, SEQ), (0, 1, 4, 2, 3))
    return (y_prompt.reshape(BATCH, SEQ, D_MODEL), y_sample.reshape(DEC_BATCH, DEC_SEQ, D_MODEL),
            to_cache(kt), to_cache(vt))
```

```python
import functools

import numpy as np
import jax
import jax.numpy as jnp
from jax import lax
from jax.experimental import pallas as pl
from jax.experimental.pallas import tpu as pltpu

D_MODEL = 1024
BATCH = 32
SEQ = 256
DEPTH = 4
DEC_BATCH = 2
DEC_SEQ = 2048
PAST_LEN = 256
GRID_W = 64
GRID_ROWS = DEC_SEQ // GRID_W
WIN_H = 8
WIN_W = 16
HEAD_DIM = 64
ATTN_WIDTH = 512
N_HEADS = 8
SGU_WIDTH = 256
SGU_GROUPS = 4
SGU_GROUP_DIM = 64
CHUNK = 128
FNET_WIDTH = 256
FNET_GROUPS = 4
FNET_GROUP_DIM = 64
N_BRANCH = 3
D_FF = 2816
N_MOD = 9
RMS_EPS = 1e-6
NEG_INF = -1e30

CTX_TOKENS = BATCH * SEQ
LAT_TOKENS = DEC_BATCH * DEC_SEQ
TOKENS = CTX_TOKENS + LAT_TOKENS
N_COND = 1 + DEC_BATCH
COND_ROWS = 8
PROJ_COLS = 3 * ATTN_WIDTH + 2 * SGU_WIDTH + FNET_WIDTH
GATE_COLS = N_BRANCH * D_MODEL
COL_BLOCK = 256
US_COL, VS_COL, F_COL = 6, 7, 8

TM = 1024
SPLIT_TM = 512
N_TILES = TOKENS // TM
CTX_TILES = CTX_TOKENS // TM
REQ_PER_TILE = TM // SEQ
FF_CHUNK = 256
VMEM_LIMIT = 56 * 1024 * 1024

BF16 = jnp.bfloat16
F32 = jnp.float32
ATTN_SCALE = np.float32(HEAD_DIM ** -0.5)


def _cond_row(i, tm=TM):
    ctx_tiles = CTX_TOKENS // tm
    return jnp.where(i < ctx_tiles, 0, 1 + (i - ctx_tiles) // (DEC_SEQ // tm))


def _ctx_tile(i, tm=TM):
    return jnp.minimum(i, CTX_TOKENS // tm - 1)


def _lat_tile(i, tm=TM):
    return jnp.maximum(i - CTX_TOKENS // tm, 0)


def _params(n_axes):
    return pltpu.CompilerParams(dimension_semantics=("arbitrary",) * n_axes,
                                vmem_limit_bytes=VMEM_LIMIT)


def _dot(a, b):
    return jnp.dot(a, b, preferred_element_type=F32)


def _dot_nt(a, b):
    return lax.dot_general(a, b, (((1,), (1,)), ((), ())), preferred_element_type=F32)


def _rms_modulate(x, g, shift, scale):
    ms = jnp.mean(x * x, axis=-1, keepdims=True)
    y = x * lax.rsqrt(ms + RMS_EPS) * g
    return y * (1.0 + scale) + shift


def _silu(x):
    return x * (1.0 / (1.0 + jnp.exp(-x)))


def _sigmoid(x):
    return 1.0 / (1.0 + jnp.exp(-x))


def _gelu_tanh(x):
    c = np.float32(np.sqrt(2.0 / np.pi))
    return x * (0.5 * (1.0 + jnp.tanh(c * (x + np.float32(0.044715) * (x * x * x)))))


MOD_COLS = N_MOD * D_MODEL
MOD_BLOCK = 2304


def _mod_kernel(c_ref, w_ref, b_ref, o_ref):
    a = _silu(c_ref[...]).astype(BF16)
    o_ref[0] = _dot(a, w_ref[0].astype(BF16)) + b_ref[0]


def _modulation(cond, w_mod, b_mod):
    return pl.pallas_call(
        _mod_kernel,
        out_shape=jax.ShapeDtypeStruct((DEPTH, COND_ROWS, MOD_COLS), F32),
        grid=(DEPTH, MOD_COLS // MOD_BLOCK),
        in_specs=[
            pl.BlockSpec((COND_ROWS, D_MODEL), lambda l, j: (0, 0)),
            pl.BlockSpec((1, D_MODEL, MOD_BLOCK), lambda l, j: (l, 0, j)),
            pl.BlockSpec((1, 1, MOD_BLOCK), lambda l, j: (l, 0, j)),
        ],
        out_specs=pl.BlockSpec((1, COND_ROWS, MOD_BLOCK), lambda l, j: (l, 0, j)),
        compiler_params=_params(2),
        name="modulation",
    )(cond, w_mod, b_mod.reshape(DEPTH, 1, MOD_COLS))


def _mod_spec(layer, tm=TM):
    return pl.BlockSpec((None, 1, N_MOD, D_MODEL), lambda i: (layer, _cond_row(i, tm), 0, 0))


def _gain_spec(layer, sub):
    return pl.BlockSpec((None, None, 1, D_MODEL), lambda i: (layer, sub, 0, 0))


STAGE_ROWS = 256
STAGE_SLOTS = 2


def _stage_weights(jobs, stage_ref, sem_ref):
    def copy(k):
        src = jobs[k][0]
        rows, cols = src.shape
        return pltpu.make_async_copy(src, stage_ref.at[k % STAGE_SLOTS, :rows, :cols], sem_ref.at[k % STAGE_SLOTS])

    copy(0).start()
    for k, (src, dst) in enumerate(jobs):
        if k + 1 < len(jobs):
            copy(k + 1).start()
        copy(k).wait()
        rows, cols = src.shape
        dst[...] = stage_ref[k % STAGE_SLOTS, :rows, :cols].astype(BF16)


def _row_chunks(src, dst, rows):
    return [(src.at[pl.ds(r, STAGE_ROWS), :], dst.at[pl.ds(r, STAGE_ROWS), :]) for r in range(0, rows, STAGE_ROWS)]


N_FF_CHUNKS = D_FF // FF_CHUNK
FFN_WEIGHTS = 3


def _ffn_kernel(*refs, mod_base, split_in, split_out, layer, ffn_sub, tm):
    n_in = 2 if split_in else 1
    x_refs, (m_ref, g_ref, wg_hbm, wu_hbm, wd_hbm, gf_ref) = refs[:n_in], refs[n_in:n_in + 6]
    n_out = 2 if split_out else 1
    o_refs = refs[n_in + 6:n_in + 6 + n_out]
    z_ref, acc_ref, wg_ref, wu_ref, wd_ref, stage_ref, stage_d_ref, sem_ref = refs[n_in + 6 + n_out:]
    first_tile = pl.program_id(0) == 0
    is_ctx = pl.program_id(0) < CTX_TOKENS // tm

    x = jnp.where(is_ctx, x_refs[0][...], x_refs[1][...]) if split_in else x_refs[0][...]
    shift = m_ref[0, mod_base:mod_base + 1, :]
    scale = m_ref[0, mod_base + 1:mod_base + 2, :]
    gate = m_ref[0, mod_base + 2:mod_base + 3, :]
    z_ref[...] = _rms_modulate(x, g_ref[...], shift, scale).astype(BF16)

    def chunk(j):
        z = z_ref[...]
        act = (_silu(_dot(z, wg_ref[j])) * _dot(z, wu_ref[j])).astype(BF16)
        return _dot(act, wd_ref[j])

    def copies(j, slot):
        cols = pl.ds(pl.multiple_of(j * FF_CHUNK, FF_CHUNK), FF_CHUNK)
        srcs = (wg_hbm.at[layer, ffn_sub, :, cols], wu_hbm.at[layer, ffn_sub, :, cols], wd_hbm.at[layer, ffn_sub, cols, :])
        dsts = (stage_ref.at[0, slot], stage_ref.at[1, slot], stage_d_ref.at[slot])
        return [pltpu.make_async_copy(src, dst, sem_ref.at[w, slot]) for w, (src, dst) in enumerate(zip(srcs, dsts))]

    @pl.when(first_tile)
    def _():
        for j in range(STAGE_SLOTS):
            for c in copies(j, j):
                c.start()
        acc_ref[...] = jnp.zeros_like(acc_ref)

        def staged_chunk(j, carry):
            slot = j % STAGE_SLOTS
            for c in copies(j, slot):
                c.wait()
            wg_ref[j] = stage_ref[0, slot].astype(BF16)
            wu_ref[j] = stage_ref[1, slot].astype(BF16)
            wd_ref[j] = stage_d_ref[slot].astype(BF16)

            @pl.when(j + STAGE_SLOTS < N_FF_CHUNKS)
            def _():
                for c in copies(j + STAGE_SLOTS, slot):
                    c.start()

            acc_ref[...] += chunk(j)
            return carry

        lax.fori_loop(0, N_FF_CHUNKS, staged_chunk, 0)

    @pl.when(jnp.logical_not(first_tile))
    def _():
        for j in range(N_FF_CHUNKS):
            if j == 0:
                acc_ref[...] = chunk(j)
            else:
                acc_ref[...] += chunk(j)

    y = x + (0.5 * gate) * acc_ref[...]
    if not split_out:
        o_refs[0][...] = y
    else:
        ms = jnp.mean(y * y, axis=-1, keepdims=True)
        y = y * lax.rsqrt(ms + RMS_EPS) * gf_ref[...]

        @pl.when(is_ctx)
        def _():
            o_refs[0][...] = y

        @pl.when(jnp.logical_not(is_ctx))
        def _():
            o_refs[1][...] = y


def _ffn(xs, mod, gains, wg, wu, wd, g_final, layer, sub, *, split_in=False, split_out=False):
    tm = SPLIT_TM if (split_in or split_out) else TM
    kern = functools.partial(_ffn_kernel, mod_base=0 if sub == 0 else 6, split_in=split_in, split_out=split_out,
                             layer=layer, ffn_sub=0 if sub == 0 else 1, tm=tm)
    row = pl.BlockSpec((tm, D_MODEL), lambda i: (i, 0))
    split_rows = [pl.BlockSpec((tm, D_MODEL), lambda i: (_ctx_tile(i, tm), 0)),
                  pl.BlockSpec((tm, D_MODEL), lambda i: (_lat_tile(i, tm), 0))]
    hbm = pl.BlockSpec(memory_space=pl.ANY)
    if split_out:
        out_shape = (jax.ShapeDtypeStruct((CTX_TOKENS, D_MODEL), F32), jax.ShapeDtypeStruct((LAT_TOKENS, D_MODEL), F32))
        out_specs = tuple(split_rows)
    else:
        out_shape = jax.ShapeDtypeStruct((TOKENS, D_MODEL), F32)
        out_specs = row
    return pl.pallas_call(
        kern,
        out_shape=out_shape,
        grid=(TOKENS // tm,),
        in_specs=(split_rows if split_in else [row]) + [
            _mod_spec(layer, tm),
            _gain_spec(layer, sub),
            hbm, hbm, hbm,
            pl.BlockSpec((1, D_MODEL), lambda i: (0, 0)),
        ],
        out_specs=out_specs,
        scratch_shapes=[
            pltpu.VMEM((tm, D_MODEL), BF16),
            pltpu.VMEM((tm, D_MODEL), F32),
            pltpu.VMEM((N_FF_CHUNKS, D_MODEL, FF_CHUNK), BF16),
            pltpu.VMEM((N_FF_CHUNKS, D_MODEL, FF_CHUNK), BF16),
            pltpu.VMEM((N_FF_CHUNKS, FF_CHUNK, D_MODEL), BF16),
            pltpu.VMEM((2, STAGE_SLOTS, D_MODEL, FF_CHUNK), F32),
            pltpu.VMEM((STAGE_SLOTS, FF_CHUNK, D_MODEL), F32),
            pltpu.SemaphoreType.DMA((FFN_WEIGHTS, STAGE_SLOTS)),
        ],
        compiler_params=_params(1),
        name="ffn",
    )(*xs, mod, gains, wg, wu, wd, g_final)


def _inproj_kernel(h_ref, m_ref, g_ref, w_hbm, kt_in, vt_in, p_ref, kt_ref, vt_ref, w_ref, stage_ref, sem_ref, z_ref,
                   *, layer):
    del kt_in, vt_in

    @pl.when(pl.program_id(0) == 0)
    def _():
        _stage_weights(_row_chunks(w_hbm.at[layer, :, pl.ds(0, PROJ_COLS)], w_ref, D_MODEL), stage_ref, sem_ref)

    shift = m_ref[0, 3:4, :]
    scale = m_ref[0, 4:5, :]
    z_ref[...] = _rms_modulate(h_ref[...], g_ref[...], shift, scale).astype(BF16)
    is_ctx = pl.program_id(0) < CTX_TILES
    q_cols, k_cols, v_cols = (slice(c * ATTN_WIDTH, (c + 1) * ATTN_WIDTH) for c in range(3))
    rest_cols = slice(3 * ATTN_WIDTH, PROJ_COLS)

    def project(cols):
        p_ref[:, cols] = _dot(z_ref[...], w_ref[:, cols])

    def transpose_requests(cols, out_ref):
        for b in range(REQ_PER_TILE):
            out_ref[b] = p_ref[b * SEQ:(b + 1) * SEQ, cols].T

    @pl.when(is_ctx)
    def _():
        project(k_cols)
        transpose_requests(k_cols, kt_ref)
        project(v_cols)
        transpose_requests(v_cols, vt_ref)
        project(q_cols)
        project(rest_cols)

    @pl.when(jnp.logical_not(is_ctx))
    def _():
        for cols in (q_cols, k_cols, v_cols, rest_cols):
            project(cols)


def _inproj(h, mod, gains, w_in, kt, vt, layer):
    cache_shape = jax.ShapeDtypeStruct((BATCH, DEPTH, ATTN_WIDTH, SEQ), F32)
    cache_spec = pl.BlockSpec((REQ_PER_TILE, None, ATTN_WIDTH, SEQ), lambda i: (_ctx_tile(i), layer, 0, 0))
    return pl.pallas_call(
        functools.partial(_inproj_kernel, layer=layer),
        out_shape=(jax.ShapeDtypeStruct((TOKENS, PROJ_COLS), F32), cache_shape, cache_shape),
        grid=(N_TILES,),
        in_specs=[
            pl.BlockSpec((TM, D_MODEL), lambda i: (i, 0)),
            _mod_spec(layer),
            _gain_spec(layer, 1),
            pl.BlockSpec(memory_space=pl.ANY),
            pl.BlockSpec(memory_space=pl.ANY),
            pl.BlockSpec(memory_space=pl.ANY),
        ],
        out_specs=(pl.BlockSpec((TM, PROJ_COLS), lambda i: (i, 0)), cache_spec, cache_spec),
        input_output_aliases={4: 1, 5: 2},
        scratch_shapes=[
            pltpu.VMEM((D_MODEL, PROJ_COLS), BF16),
            pltpu.VMEM((STAGE_SLOTS, STAGE_ROWS, PROJ_COLS), F32),
            pltpu.SemaphoreType.DMA((STAGE_SLOTS,)),
            pltpu.VMEM((TM, D_MODEL), BF16),
        ],
        compiler_params=_params(1),
        name="inproj",
    )(h, mod, gains, w_in, kt, vt)


HEAD_PAIRS = N_HEADS // 2
PAIR_WIDTH = 2 * HEAD_DIM
PAIR_COLS = [slice(j * PAIR_WIDTH, (j + 1) * PAIR_WIDTH) for j in range(HEAD_PAIRS)]


def _first_of_pair(rows):
    return lax.broadcasted_iota(jnp.int32, (rows, PAIR_WIDTH), 1) < HEAD_DIM


def _pair_queries(q):
    first = _first_of_pair(q.shape[0])
    zero = jnp.zeros_like(q)
    return jnp.concatenate([jnp.where(first, q, zero), jnp.where(first, zero, q)], axis=0)


def _pair_outputs(o):
    m = o.shape[0] // 2
    return jnp.where(_first_of_pair(m), o[:m], o[m:])


def _ctx_attn_kernel(q_ref, kt_ref, v_ref, o_ref):
    def one_request(b, carry):
        rows = pl.ds(pl.multiple_of(b * SEQ, SEQ), SEQ)
        q = (q_ref[rows, :] * ATTN_SCALE).astype(BF16)
        kt = kt_ref[b].astype(BF16)
        v = v_ref[rows, :].astype(BF16)
        s = jnp.concatenate([_dot(_pair_queries(q[:, c]), kt[c, :]) for c in PAIR_COLS], axis=0)
        e = jnp.exp(s - jnp.max(s, axis=-1, keepdims=True))
        inv = 1.0 / jnp.sum(e, axis=-1, keepdims=True)
        p = e.astype(BF16)
        outs = []
        for j, c in enumerate(PAIR_COLS):
            pair = slice(2 * j * SEQ, 2 * (j + 1) * SEQ)
            outs.append(_pair_outputs(_dot(p[pair, :], v[:, c]) * inv[pair, :]))
        o_ref[rows, :] = jnp.concatenate(outs, axis=-1).astype(BF16)
        return carry

    lax.fori_loop(0, REQ_PER_TILE, one_request, 0, unroll=2)


def _ctx_attention(proj, kt, layer):
    return pl.pallas_call(
        _ctx_attn_kernel,
        out_shape=jax.ShapeDtypeStruct((CTX_TOKENS, ATTN_WIDTH), BF16),
        grid=(CTX_TILES,),
        in_specs=[
            pl.BlockSpec((TM, ATTN_WIDTH), lambda i: (i, 0)),
            pl.BlockSpec((REQ_PER_TILE, None, ATTN_WIDTH, SEQ), lambda i: (i, layer, 0, 0)),
            pl.BlockSpec((TM, ATTN_WIDTH), lambda i: (i, 2)),
        ],
        out_specs=pl.BlockSpec((TM, ATTN_WIDTH), lambda i: (i, 0)),
        compiler_params=_params(1),
        name="ctx_attention",
    )(proj, kt, proj)


N_DROW = 2 * WIN_H - 1
N_DCOL = 2 * WIN_W - 1
N_DROW_PAIRS = N_DROW - 1
KEY_ROWS = WIN_H
KEYS_LOCAL = KEY_ROWS * GRID_W
RPB_LANES = 2 * GRID_W


def _build_bias_table(rpb_ref, tab_ref):
    first = _first_of_pair(GRID_W)
    for h in range(N_HEADS):
        for d in range(N_DROW_PAIRS):
            lo = jnp.broadcast_to(rpb_ref[h, d:d + 1, :], (GRID_W, RPB_LANES))
            hi = jnp.broadcast_to(rpb_ref[h, d + 1:d + 2, :], (GRID_W, RPB_LANES))
            lo = pltpu.roll(lo, RPB_LANES - (WIN_W - 1), 1, stride=1, stride_axis=0)
            hi = pltpu.roll(hi, GRID_W - (WIN_W - 1), 1, stride=1, stride_axis=0)
            tab_ref[h, d] = jnp.where(first, lo, hi)


def _natten_kernel(q_ref, k_ref, v_ref, kc_ref, vc_ref, rpb_ref, o_ref, qb_ref, kb_ref, vb_ref, tab_ref):
    @pl.when(pl.program_id(0) == 0)
    def _():
        _build_bias_table(rpb_ref, tab_ref)

    qb_ref[...] = (q_ref[...] * ATTN_SCALE).astype(BF16)
    kb_ref[...] = k_ref[...].astype(BF16)
    vb_ref[...] = v_ref[...].astype(BF16)
    kc = kc_ref[...].astype(BF16)
    vc = vc_ref[...].astype(BF16)
    qcol = lax.broadcasted_iota(jnp.int32, (N_HEADS * GRID_W, KEYS_LOCAL), 0) % GRID_W
    kcol = lax.broadcasted_iota(jnp.int32, (N_HEADS * GRID_W, KEYS_LOCAL), 1) % GRID_W
    start = jnp.clip(qcol - WIN_W // 2, 0, GRID_W - WIN_W)
    col_ok = (kcol >= start) & (kcol < start + WIN_W)

    def one_row(r, carry):
        r0 = jnp.clip(r - KEY_ROWS // 2, 0, GRID_ROWS - KEY_ROWS)
        d0 = r0 - r + (WIN_H - 1)
        qrows = pl.ds(pl.multiple_of(r * GRID_W, GRID_W), GRID_W)
        krows = pl.ds(pl.multiple_of(r0 * GRID_W, GRID_W), KEYS_LOCAL)
        q = qb_ref[qrows, :]
        k = jnp.concatenate([kb_ref[krows, :], kc], axis=0)
        v = jnp.concatenate([vb_ref[krows, :], vc], axis=0)
        s = jnp.concatenate([_dot_nt(_pair_queries(q[:, c]), k[:, c]) for c in PAIR_COLS], axis=0)
        bias = jnp.concatenate(
            [jnp.concatenate([tab_ref[h, d0 + 2 * i] for i in range(KEY_ROWS // 2)], axis=-1)
             for h in range(N_HEADS)], axis=0)
        s_loc = jnp.where(col_ok, s[:, :KEYS_LOCAL] + bias, NEG_INF)
        s_ctx = s[:, KEYS_LOCAL:]
        mx = jnp.maximum(jnp.max(s_loc, axis=-1, keepdims=True), jnp.max(s_ctx, axis=-1, keepdims=True))
        e_loc = jnp.exp(s_loc - mx)
        e_ctx = jnp.exp(s_ctx - mx)
        inv = 1.0 / (jnp.sum(e_loc, axis=-1, keepdims=True) + jnp.sum(e_ctx, axis=-1, keepdims=True))
        p = jnp.concatenate([e_loc, e_ctx], axis=-1).astype(BF16)
        outs = []
        for j, c in enumerate(PAIR_COLS):
            pair = slice(2 * j * GRID_W, 2 * (j + 1) * GRID_W)
            outs.append(_pair_outputs(_dot(p[pair, :], v[:, c]) * inv[pair, :]))
        o_ref[qrows, :] = jnp.concatenate(outs, axis=-1).astype(BF16)
        return carry

    lax.fori_loop(0, GRID_ROWS, one_row, 0, unroll=2)


def _natten(proj, cache_k, cache_v, rpb_pad, layer):
    lat0 = CTX_TOKENS // DEC_SEQ
    spec = lambda c: pl.BlockSpec((DEC_SEQ, ATTN_WIDTH), lambda b, c=c: (lat0 + b, c))
    cache_spec = pl.BlockSpec((None, None, PAST_LEN, ATTN_WIDTH), lambda b: (b, layer, 0, 0))
    return pl.pallas_call(
        _natten_kernel,
        out_shape=jax.ShapeDtypeStruct((LAT_TOKENS, ATTN_WIDTH), BF16),
        grid=(DEC_BATCH,),
        in_specs=[spec(0), spec(1), spec(2), cache_spec, cache_spec,
                  pl.BlockSpec((None, N_HEADS, N_DROW, RPB_LANES), lambda b: (layer, 0, 0, 0))],
        out_specs=pl.BlockSpec((DEC_SEQ, ATTN_WIDTH), lambda b: (b, 0)),
        scratch_shapes=[pltpu.VMEM((DEC_SEQ, ATTN_WIDTH), BF16)] * 3
        + [pltpu.VMEM((N_HEADS, N_DROW_PAIRS, GRID_W, RPB_LANES), F32)],
        compiler_params=_params(1),
        name="natten",
    )(proj, proj, proj, cache_k, cache_v, rpb_pad)


def _sgu_kernel(u_ref, v_ref, g_ref, w_ref, b_ref, o_ref):
    u = _gelu_tanh(u_ref[...])
    v = _gelu_tanh(v_ref[...])
    ms = jnp.mean(v * v, axis=-1, keepdims=True)
    vn = v * lax.rsqrt(ms + RMS_EPS) * g_ref[...]
    group = lax.broadcasted_iota(jnp.int32, (CHUNK, SGU_WIDTH), 1) // SGU_GROUP_DIM
    w = w_ref[...].astype(BF16)
    bias = b_ref[...]
    for n in range(TM // CHUNK):
        rows = slice(n * CHUNK, (n + 1) * CHUNK)
        x = vn[rows, :]
        stacked = jnp.concatenate(
            [jnp.where(group == g, x, 0.0) for g in range(SGU_GROUPS)], axis=0).astype(BF16)
        mixed = _dot(w, stacked) + bias
        o_ref[rows, :] = (u[rows, :] * mixed).astype(BF16)


def _sgu(proj, sgu_norm, w_cat, bias_tile, layer):
    return pl.pallas_call(
        _sgu_kernel,
        out_shape=jax.ShapeDtypeStruct((TOKENS, SGU_WIDTH), BF16),
        grid=(N_TILES,),
        in_specs=[
            pl.BlockSpec((TM, SGU_WIDTH), lambda i: (i, US_COL)),
            pl.BlockSpec((TM, SGU_WIDTH), lambda i: (i, VS_COL)),
            pl.BlockSpec((None, 1, SGU_WIDTH), lambda i: (layer, 0, 0)),
            pl.BlockSpec((None, CHUNK, SGU_GROUPS * CHUNK), lambda i: (layer, 0, 0)),
            pl.BlockSpec((None, CHUNK, SGU_WIDTH), lambda i: (layer, 0, 0)),
        ],
        out_specs=pl.BlockSpec((TM, SGU_WIDTH), lambda i: (i, 0)),
        compiler_params=_params(1),
        name="sgu",
    )(proj, proj, sgu_norm, w_cat, bias_tile)


def _dft_tables(n):
    k = np.arange(n, dtype=np.int64)
    ang = 2.0 * np.pi * ((k[:, None] * k[None, :]) % n).astype(np.float64) / n
    return np.cos(ang), np.sin(ang)


def _channel_tables():
    c, s = _dft_tables(FNET_GROUP_DIM)
    eye = np.eye(FNET_GROUPS)
    return np.concatenate([np.kron(eye, c), np.kron(eye, s)], axis=1)


_CHAN_TABLE = _channel_tables().astype(np.float32)
_CTX_POS_TABLE = np.concatenate(_dft_tables(SEQ), axis=1).astype(np.float32)
_LAT_POS_TABLE = np.concatenate(_dft_tables(DEC_SEQ), axis=1).astype(np.float32)
FNET_CTX_SCALE = np.float32(1.0 / np.sqrt(SEQ * FNET_GROUP_DIM))
FNET_LAT_SCALE = np.float32(1.0 / np.sqrt(DEC_SEQ * FNET_GROUP_DIM))


def _fnet_ctx_kernel(f_ref, chan_ref, pos_ref, o_ref):
    y = _dot(f_ref[...].astype(BF16), chan_ref[...].astype(BF16))
    pos = pos_ref[...].astype(BF16)
    for b in range(TM // SEQ):
        rows = slice(b * SEQ, (b + 1) * SEQ)
        yc = y[rows, :FNET_WIDTH].astype(BF16)
        ys = y[rows, FNET_WIDTH:].astype(BF16)
        out = _dot(pos[:, :SEQ], yc) - _dot(pos[:, SEQ:], ys)
        o_ref[rows, :] = (out * FNET_CTX_SCALE).astype(BF16)


def _fnet_ctx(proj, chan, pos):
    whole = lambda i: (0, 0)
    return pl.pallas_call(
        _fnet_ctx_kernel,
        out_shape=jax.ShapeDtypeStruct((CTX_TOKENS, FNET_WIDTH), BF16),
        grid=(CTX_TILES,),
        in_specs=[
            pl.BlockSpec((TM, FNET_WIDTH), lambda i: (i, F_COL)),
            pl.BlockSpec((FNET_WIDTH, 2 * FNET_WIDTH), whole),
            pl.BlockSpec((SEQ, 2 * SEQ), whole),
        ],
        out_specs=pl.BlockSpec((TM, FNET_WIDTH), lambda i: (i, 0)),
        compiler_params=_params(1),
        name="fnet_ctx",
    )(proj, chan, pos)


FNET_LAT_ROWS = 256


def _fnet_lat_kernel(f0_ref, f1_ref, chan_ref, pos_ref, o_ref, yc_ref, ys_ref):
    @pl.when(pl.program_id(0) == 0)
    def _():
        for b, f_ref in enumerate((f0_ref, f1_ref)):
            y = _dot(f_ref[...].astype(BF16), chan_ref[...].astype(BF16))
            cols = slice(b * FNET_WIDTH, (b + 1) * FNET_WIDTH)
            yc_ref[:, cols] = y[:, :FNET_WIDTH].astype(BF16)
            ys_ref[:, cols] = y[:, FNET_WIDTH:].astype(BF16)

    out = (_dot(pos_ref[:, :DEC_SEQ].astype(BF16), yc_ref[...])
           - _dot(pos_ref[:, DEC_SEQ:].astype(BF16), ys_ref[...]))
    out = out * FNET_LAT_SCALE
    for b in range(DEC_BATCH):
        o_ref[b] = out[:, b * FNET_WIDTH:(b + 1) * FNET_WIDTH].astype(BF16)


def _fnet_lat(proj, chan, pos):
    lat0 = CTX_TOKENS // DEC_SEQ
    return pl.pallas_call(
        _fnet_lat_kernel,
        out_shape=jax.ShapeDtypeStruct((DEC_BATCH, DEC_SEQ, FNET_WIDTH), BF16),
        grid=(DEC_SEQ // FNET_LAT_ROWS,),
        in_specs=[
            pl.BlockSpec((DEC_SEQ, FNET_WIDTH), lambda j: (lat0, F_COL)),
            pl.BlockSpec((DEC_SEQ, FNET_WIDTH), lambda j: (lat0 + 1, F_COL)),
            pl.BlockSpec((FNET_WIDTH, 2 * FNET_WIDTH), lambda j: (0, 0)),
            pl.BlockSpec((FNET_LAT_ROWS, 2 * DEC_SEQ), lambda j: (j, 0)),
        ],
        out_specs=pl.BlockSpec((DEC_BATCH, FNET_LAT_ROWS, FNET_WIDTH), lambda j: (0, j, 0)),
        scratch_shapes=[pltpu.VMEM((DEC_SEQ, DEC_BATCH * FNET_WIDTH), BF16)] * 2,
        compiler_params=_params(1),
        name="fnet_lat",
    )(proj, proj, chan, pos)


def _merge_kernel(h_ref, m_ref, g_ref, w_in_hbm, bg_ref, oac_ref, oal_ref, ob_ref, occ_ref, ocl_ref,
                  pa_hbm, pb_hbm, pc_hbm, wo_hbm, o_ref, wg_ref, pa_ref, pb_ref, pc_ref, wo_ref, stage_ref, sem_ref,
                  *, layer):
    @pl.when(pl.program_id(0) == 0)
    def _():
        jobs = (_row_chunks(w_in_hbm.at[layer, :, pl.ds(PROJ_COLS, GATE_COLS)], wg_ref, D_MODEL)
                + _row_chunks(pa_hbm.at[layer], pa_ref, ATTN_WIDTH)
                + _row_chunks(pb_hbm.at[layer], pb_ref, SGU_WIDTH)
                + _row_chunks(pc_hbm.at[layer], pc_ref, FNET_WIDTH)
                + _row_chunks(wo_hbm.at[layer], wo_ref, D_MODEL))
        _stage_weights(jobs, stage_ref, sem_ref)

    ctx = pl.program_id(0) < CTX_TILES
    x = h_ref[...]
    shift = m_ref[0, 3:4, :]
    scale = m_ref[0, 4:5, :]
    gate = m_ref[0, 5:6, :]
    z = _rms_modulate(x, g_ref[...], shift, scale).astype(BF16)
    oa = jnp.where(ctx, oac_ref[...], oal_ref[...])
    oc = jnp.where(ctx, occ_ref[...], ocl_ref[...])
    branches = ((oa, pa_ref), (ob_ref[...], pb_ref), (oc, pc_ref))
    mix = None
    for j, (o, p_ref) in enumerate(branches):
        cols = slice(j * D_MODEL, (j + 1) * D_MODEL)
        gj = _sigmoid(_dot(z, wg_ref[:, cols]) + bg_ref[j:j + 1, :])
        term = gj * _dot(o, p_ref[...])
        mix = term if mix is None else mix + term
    o_ref[...] = x + gate * _dot(mix.astype(BF16), wo_ref[...])


def _merge(h, mod, gains, w_in, b_gate, oa_ctx, oa_lat, ob, oc_ctx, oc_lat, p_attn, p_sgu, p_fnet, w_out, layer):
    lw = lambda i: (layer, 0, 0)
    ctx_tile = lambda i: (_ctx_tile(i), 0)
    lat_tile = lambda i: (_lat_tile(i), 0)
    hbm = pl.BlockSpec(memory_space=pl.ANY)
    return pl.pallas_call(
        functools.partial(_merge_kernel, layer=layer),
        out_shape=jax.ShapeDtypeStruct((TOKENS, D_MODEL), F32),
        grid=(N_TILES,),
        in_specs=[
            pl.BlockSpec((TM, D_MODEL), lambda i: (i, 0)),
            _mod_spec(layer),
            _gain_spec(layer, 1),
            hbm,
            pl.BlockSpec((None, N_BRANCH, D_MODEL), lw),
            pl.BlockSpec((TM, ATTN_WIDTH), ctx_tile),
            pl.BlockSpec((TM, ATTN_WIDTH), lat_tile),
            pl.BlockSpec((TM, SGU_WIDTH), lambda i: (i, 0)),
            pl.BlockSpec((TM, FNET_WIDTH), ctx_tile),
            pl.BlockSpec((TM, FNET_WIDTH), lat_tile),
            hbm, hbm, hbm, hbm,
        ],
        out_specs=pl.BlockSpec((TM, D_MODEL), lambda i: (i, 0)),
        scratch_shapes=[
            pltpu.VMEM((D_MODEL, GATE_COLS), BF16),
            pltpu.VMEM((ATTN_WIDTH, D_MODEL), BF16),
            pltpu.VMEM((SGU_WIDTH, D_MODEL), BF16),
            pltpu.VMEM((FNET_WIDTH, D_MODEL), BF16),
            pltpu.VMEM((D_MODEL, D_MODEL), BF16),
            pltpu.VMEM((STAGE_SLOTS, STAGE_ROWS, GATE_COLS), F32),
            pltpu.SemaphoreType.DMA((STAGE_SLOTS,)),
        ],
        compiler_params=_params(1),
        name="merge",
    )(h, mod, gains, w_in, b_gate, oa_ctx, oa_lat, ob, oc_ctx, oc_lat, p_attn, p_sgu, p_fnet, w_out)


def kernel(x_prompt, x_sample, cache_k, cache_v, c, c_ctx, w_mod, b_mod, g_norm, ffn_w_gate, ffn_w_up,
           ffn_w_down, w_in, b_gate, rpb, sgu_norm, sgu_w, sgu_b, p_attn, p_sgu, p_fnet, w_out, g_final):
    cond = jnp.concatenate([c_ctx[None, :], c, jnp.zeros((COND_ROWS - N_COND, D_MODEL), F32)], axis=0)
    mod = _modulation(cond, w_mod, b_mod).reshape(DEPTH, COND_ROWS, N_MOD, D_MODEL)
    rpb_pad = jnp.pad(rpb, ((0, 0), (0, 0), (0, 0), (0, RPB_LANES - N_DCOL)))
    cache_k = cache_k.reshape(DEC_BATCH, DEPTH, PAST_LEN, ATTN_WIDTH)
    cache_v = cache_v.reshape(DEC_BATCH, DEPTH, PAST_LEN, ATTN_WIDTH)
    gains = g_norm.reshape(DEPTH, 3, 1, D_MODEL)
    g_final2 = g_final.reshape(1, D_MODEL)

    sgu_gain = sgu_norm.reshape(DEPTH, 1, SGU_WIDTH)
    w_cat = jnp.transpose(sgu_w, (0, 2, 1, 3)).reshape(DEPTH, CHUNK, SGU_GROUPS * CHUNK)
    bias_tile = jnp.repeat(jnp.transpose(sgu_b, (0, 2, 1)), SGU_GROUP_DIM, axis=2)

    chan = jnp.asarray(_CHAN_TABLE)
    pos_ctx = jnp.asarray(_CTX_POS_TABLE)
    pos_lat = jnp.asarray(_LAT_POS_TABLE)

    kt = jnp.zeros((BATCH, DEPTH, ATTN_WIDTH, SEQ), F32)
    vt = jnp.zeros((BATCH, DEPTH, ATTN_WIDTH, SEQ), F32)

    xs = (x_prompt.reshape(CTX_TOKENS, D_MODEL), x_sample.reshape(LAT_TOKENS, D_MODEL))
    for l in range(DEPTH):
        h = _ffn(xs, mod, gains, ffn_w_gate, ffn_w_up, ffn_w_down, g_final2, l, 0, split_in=(l == 0))
        proj, kt, vt = _inproj(h, mod, gains, w_in, kt, vt, l)
        oa_ctx = _ctx_attention(proj, kt, l)
        oa_lat = _natten(proj, cache_k, cache_v, rpb_pad, l)
        ob = _sgu(proj, sgu_gain, w_cat, bias_tile, l)
        oc_ctx = _fnet_ctx(proj, chan, pos_ctx)
        oc_lat = _fnet_lat(proj, chan, pos_lat).reshape(LAT_TOKENS, FNET_WIDTH)
        h = _merge(h, mod, gains, w_in, b_gate, oa_ctx, oa_lat, ob, oc_ctx, oc_lat, p_attn, p_sgu, p_fnet, w_out, l)
        xs = _ffn((h,), mod, gains, ffn_w_gate, ffn_w_up, ffn_w_down, g_final2, l, 2, split_out=(l == DEPTH - 1))
        xs = xs if l == DEPTH - 1 else (xs,)

    y_prompt, y_sample = xs
    to_cache = lambda t: jnp.transpose(t.reshape(BATCH, DEPTH, N_HEADS, HEAD_DIM, SEQ), (0, 1, 4, 2, 3))
    return (y_prompt.reshape(BATCH, SEQ, D_MODEL), y_sample.reshape(DEC_BATCH, DEC_SEQ, D_MODEL),
            to_cache(kt), to_cache(vt))
```

```python
import functools

import numpy as np
import jax
import jax.numpy as jnp
from jax import lax
from jax.experimental import pallas as pl
from jax.experimental.pallas import tpu as pltpu

D_MODEL = 1024
BATCH = 32
SEQ = 256
DEPTH = 4
DEC_BATCH = 2
DEC_SEQ = 2048
PAST_LEN = 256
GRID_W = 64
GRID_ROWS = DEC_SEQ // GRID_W
WIN_H = 8
WIN_W = 16
HEAD_DIM = 64
ATTN_WIDTH = 512
N_HEADS = 8
SGU_WIDTH = 256
SGU_GROUPS = 4
SGU_GROUP_DIM = 64
CHUNK = 128
FNET_WIDTH = 256
FNET_GROUPS = 4
FNET_GROUP_DIM = 64
N_BRANCH = 3
D_FF = 2816
N_MOD = 9
RMS_EPS = 1e-6
NEG_INF = -1e30

CTX_TOKENS = BATCH * SEQ
LAT_TOKENS = DEC_BATCH * DEC_SEQ
TOKENS = CTX_TOKENS + LAT_TOKENS
N_COND = 1 + DEC_BATCH
COND_ROWS = 8
PROJ_COLS = 3 * ATTN_WIDTH + 2 * SGU_WIDTH + FNET_WIDTH
GATE_COLS = N_BRANCH * D_MODEL

TM = 1024
SPLIT_TM = 512
N_TILES = TOKENS // TM
CTX_TILES = CTX_TOKENS // TM
REQ_PER_TILE = TM // SEQ
FF_CHUNK = 256
VMEM_LIMIT = 56 * 1024 * 1024

BF16 = jnp.bfloat16
F32 = jnp.float32
ATTN_SCALE = np.float32(HEAD_DIM ** -0.5)


def _cond_row(i, tm=TM):
    ctx_tiles = CTX_TOKENS // tm
    return jnp.where(i < ctx_tiles, 0, 1 + (i - ctx_tiles) // (DEC_SEQ // tm))


def _ctx_tile(i, tm=TM):
    return jnp.minimum(i, CTX_TOKENS // tm - 1)


def _lat_tile(i, tm=TM):
    return jnp.maximum(i - CTX_TOKENS // tm, 0)


def _params(n_axes):
    return pltpu.CompilerParams(dimension_semantics=("arbitrary",) * n_axes,
                                vmem_limit_bytes=VMEM_LIMIT)


def _dot(a, b):
    return jnp.dot(a, b, preferred_element_type=F32)


def _dot_nt(a, b):
    return lax.dot_general(a, b, (((1,), (1,)), ((), ())), preferred_element_type=F32)


def _rms_modulate(x, g, shift, scale):
    ms = jnp.mean(x * x, axis=-1, keepdims=True)
    y = x * lax.rsqrt(ms + RMS_EPS) * g
    return y * (1.0 + scale) + shift


def _silu(x):
    return x * (1.0 / (1.0 + jnp.exp(-x)))


def _sigmoid(x):
    return 1.0 / (1.0 + jnp.exp(-x))


def _gelu_tanh(x):
    c = np.float32(np.sqrt(2.0 / np.pi))
    return x * (0.5 * (1.0 + jnp.tanh(c * (x + np.float32(0.044715) * (x * x * x)))))


MOD_COLS = N_MOD * D_MODEL
MOD_BLOCK = 2304


def _mod_kernel(c_ref, w_ref, b_ref, o_ref):
    a = _silu(c_ref[...]).astype(BF16)
    o_ref[0] = _dot(a, w_ref[0].astype(BF16)) + b_ref[0]


def _modulation(cond, w_mod, b_mod):
    return pl.pallas_call(
        _mod_kernel,
        out_shape=jax.ShapeDtypeStruct((DEPTH, COND_ROWS, MOD_COLS), F32),
        grid=(DEPTH, MOD_COLS // MOD_BLOCK),
        in_specs=[
            pl.BlockSpec((COND_ROWS, D_MODEL), lambda l, j: (0, 0)),
            pl.BlockSpec((1, D_MODEL, MOD_BLOCK), lambda l, j: (l, 0, j)),
            pl.BlockSpec((1, 1, MOD_BLOCK), lambda l, j: (l, 0, j)),
        ],
        out_specs=pl.BlockSpec((1, COND_ROWS, MOD_BLOCK), lambda l, j: (l, 0, j)),
        compiler_params=_params(2),
        name="modulation",
    )(cond, w_mod, b_mod.reshape(DEPTH, 1, MOD_COLS))


def _mod_spec(layer, tm=TM):
    return pl.BlockSpec((None, 1, N_MOD, D_MODEL), lambda i: (layer, _cond_row(i, tm), 0, 0))


def _gain_spec(layer, sub):
    return pl.BlockSpec((None, None, 1, D_MODEL), lambda i: (layer, sub, 0, 0))


STAGE_ROWS = 256
STAGE_SLOTS = 2


def _stage_weights(jobs, stage_ref, sem_ref):
    def copy(k):
        src = jobs[k][0]
        rows, cols = src.shape
        return pltpu.make_async_copy(src, stage_ref.at[k % STAGE_SLOTS, :rows, :cols], sem_ref.at[k % STAGE_SLOTS])

    copy(0).start()
    for k, (src, dst) in enumerate(jobs):
        if k + 1 < len(jobs):
            copy(k + 1).start()
        copy(k).wait()
        rows, cols = src.shape
        dst[...] = stage_ref[k % STAGE_SLOTS, :rows, :cols].astype(BF16)


def _row_chunks(src, dst, rows):
    return [(src.at[pl.ds(r, STAGE_ROWS), :], dst.at[pl.ds(r, STAGE_ROWS), :]) for r in range(0, rows, STAGE_ROWS)]


N_FF_CHUNKS = D_FF // FF_CHUNK
FFN_WEIGHTS = 3


def _ffn_kernel(*refs, mod_base, split_in, split_out, layer, ffn_sub, tm):
    n_in = 2 if split_in else 1
    x_refs, (m_ref, g_ref, wg_hbm, wu_hbm, wd_hbm, gf_ref) = refs[:n_in], refs[n_in:n_in + 6]
    n_out = 2 if split_out else 1
    o_refs = refs[n_in + 6:n_in + 6 + n_out]
    z_ref, acc_ref, wg_ref, wu_ref, wd_ref, stage_ref, stage_d_ref, sem_ref = refs[n_in + 6 + n_out:]
    first_tile = pl.program_id(0) == 0
    is_ctx = pl.program_id(0) < CTX_TOKENS // tm

    def chunk(j):
        z = z_ref[...]
        act = (_silu(_dot(z, wg_ref[j])) * _dot(z, wu_ref[j])).astype(BF16)
        return _dot(act, wd_ref[j])

    def copies(j, slot):
        cols = pl.ds(pl.multiple_of(j * FF_CHUNK, FF_CHUNK), FF_CHUNK)
        srcs = (wg_hbm.at[layer, ffn_sub, :, cols], wu_hbm.at[layer, ffn_sub, :, cols], wd_hbm.at[layer, ffn_sub, cols, :])
        dsts = (stage_ref.at[0, slot], stage_ref.at[1, slot], stage_d_ref.at[slot])
        return [pltpu.make_async_copy(src, dst, sem_ref.at[w, slot]) for w, (src, dst) in enumerate(zip(srcs, dsts))]

    def staged_chunks():
        for j in range(STAGE_SLOTS):
            for c in copies(j, j):
                c.start()
        acc_ref[...] = jnp.zeros_like(acc_ref)

        def staged_chunk(j, carry):
            slot = j % STAGE_SLOTS
            for c in copies(j, slot):
                c.wait()
            wg_ref[j] = stage_ref[0, slot].astype(BF16)
            wu_ref[j] = stage_ref[1, slot].astype(BF16)
            wd_ref[j] = stage_d_ref[slot].astype(BF16)

            @pl.when(j + STAGE_SLOTS < N_FF_CHUNKS)
            def _():
                for c in copies(j + STAGE_SLOTS, slot):
                    c.start()

            acc_ref[...] += chunk(j)
            return carry

        lax.fori_loop(0, N_FF_CHUNKS, staged_chunk, 0)

    def unrolled_chunks():
        for j in range(N_FF_CHUNKS):
            if j == 0:
                acc_ref[...] = chunk(j)
            else:
                acc_ref[...] += chunk(j)

    def tile(run_chunks):
        x = jnp.where(is_ctx, x_refs[0][...], x_refs[1][...]) if split_in else x_refs[0][...]
        shift = m_ref[0, mod_base:mod_base + 1, :]
        scale = m_ref[0, mod_base + 1:mod_base + 2, :]
        gate = m_ref[0, mod_base + 2:mod_base + 3, :]
        z_ref[...] = _rms_modulate(x, g_ref[...], shift, scale).astype(BF16)
        run_chunks()
        y = x + (0.5 * gate) * acc_ref[...]
        if not split_out:
            o_refs[0][...] = y
        else:
            ms = jnp.mean(y * y, axis=-1, keepdims=True)
            y = y * lax.rsqrt(ms + RMS_EPS) * gf_ref[...]

            @pl.when(is_ctx)
            def _():
                o_refs[0][...] = y

            @pl.when(jnp.logical_not(is_ctx))
            def _():
                o_refs[1][...] = y

    @pl.when(first_tile)
    def _():
        tile(staged_chunks)

    @pl.when(jnp.logical_not(first_tile))
    def _():
        tile(unrolled_chunks)


def _ffn(xs, mod, gains, wg, wu, wd, g_final, layer, sub, *, split_in=False, split_out=False):
    tm = SPLIT_TM if (split_in or split_out) else TM
    kern = functools.partial(_ffn_kernel, mod_base=0 if sub == 0 else 6, split_in=split_in, split_out=split_out,
                             layer=layer, ffn_sub=0 if sub == 0 else 1, tm=tm)
    row = pl.BlockSpec((tm, D_MODEL), lambda i: (i, 0))
    split_rows = [pl.BlockSpec((tm, D_MODEL), lambda i: (_ctx_tile(i, tm), 0)),
                  pl.BlockSpec((tm, D_MODEL), lambda i: (_lat_tile(i, tm), 0))]
    hbm = pl.BlockSpec(memory_space=pl.ANY)
    if split_out:
        out_shape = (jax.ShapeDtypeStruct((CTX_TOKENS, D_MODEL), F32), jax.ShapeDtypeStruct((LAT_TOKENS, D_MODEL), F32))
        out_specs = tuple(split_rows)
    else:
        out_shape = jax.ShapeDtypeStruct((TOKENS, D_MODEL), F32)
        out_specs = row
    return pl.pallas_call(
        kern,
        out_shape=out_shape,
        grid=(TOKENS // tm,),
        in_specs=(split_rows if split_in else [row]) + [
            _mod_spec(layer, tm),
            _gain_spec(layer, sub),
            hbm, hbm, hbm,
            pl.BlockSpec((1, D_MODEL), lambda i: (0, 0)),
        ],
        out_specs=out_specs,
        scratch_shapes=[
            pltpu.VMEM((tm, D_MODEL), BF16),
            pltpu.VMEM((tm, D_MODEL), F32),
            pltpu.VMEM((N_FF_CHUNKS, D_MODEL, FF_CHUNK), BF16),
            pltpu.VMEM((N_FF_CHUNKS, D_MODEL, FF_CHUNK), BF16),
            pltpu.VMEM((N_FF_CHUNKS, FF_CHUNK, D_MODEL), BF16),
            pltpu.VMEM((2, STAGE_SLOTS, D_MODEL, FF_CHUNK), F32),
            pltpu.VMEM((STAGE_SLOTS, FF_CHUNK, D_MODEL), F32),
            pltpu.SemaphoreType.DMA((FFN_WEIGHTS, STAGE_SLOTS)),
        ],
        compiler_params=_params(1),
        name="ffn",
    )(*xs, mod, gains, wg, wu, wd, g_final)


def _inproj_kernel(h_ref, m_ref, g_ref, w_hbm, kt_in, vt_in, pa_ref, pb_ref, pf_ref, kt_ref, vt_ref,
                   w_ref, stage_ref, sem_ref, z_ref, kf_ref, vf_ref, *, layer):
    del kt_in, vt_in

    @pl.when(pl.program_id(0) == 0)
    def _():
        _stage_weights(_row_chunks(w_hbm.at[layer, :, pl.ds(0, PROJ_COLS)], w_ref, D_MODEL), stage_ref, sem_ref)

    is_ctx = pl.program_id(0) < CTX_TILES
    q_cols, k_cols, v_cols = (slice(c * ATTN_WIDTH, (c + 1) * ATTN_WIDTH) for c in range(3))
    rest_cols = slice(3 * ATTN_WIDTH, PROJ_COLS)

    def project(cols):
        return _dot(z_ref[...], w_ref[:, cols])

    def tile(write_cache):
        shift = m_ref[0, 3:4, :]
        scale = m_ref[0, 4:5, :]
        z_ref[...] = _rms_modulate(h_ref[...], g_ref[...], shift, scale).astype(BF16)
        for cols, full_ref, cache_ref in ((k_cols, kf_ref, kt_ref), (v_cols, vf_ref, vt_ref)):
            y = project(cols)
            pa_ref[:, cols] = y.astype(BF16)
            if write_cache:
                full_ref[...] = y
                for b in range(REQ_PER_TILE):
                    cache_ref[b] = full_ref[b * SEQ:(b + 1) * SEQ, :].T
        pa_ref[:, q_cols] = (project(q_cols) * ATTN_SCALE).astype(BF16)
        rest = project(rest_cols)
        pb_ref[...] = rest[:, :2 * SGU_WIDTH]
        pf_ref[...] = rest[:, 2 * SGU_WIDTH:].astype(BF16)

    @pl.when(is_ctx)
    def _():
        tile(True)

    @pl.when(jnp.logical_not(is_ctx))
    def _():
        tile(False)


def _inproj(h, mod, gains, w_in, kt, vt, layer):
    cache_shape = jax.ShapeDtypeStruct((BATCH, DEPTH, ATTN_WIDTH, SEQ), F32)
    cache_spec = pl.BlockSpec((REQ_PER_TILE, None, ATTN_WIDTH, SEQ), lambda i: (_ctx_tile(i), layer, 0, 0))
    row = lambda width: pl.BlockSpec((TM, width), lambda i: (i, 0))
    return pl.pallas_call(
        functools.partial(_inproj_kernel, layer=layer),
        out_shape=(jax.ShapeDtypeStruct((TOKENS, 3 * ATTN_WIDTH), BF16),
                   jax.ShapeDtypeStruct((TOKENS, 2 * SGU_WIDTH), F32),
                   jax.ShapeDtypeStruct((TOKENS, FNET_WIDTH), BF16), cache_shape, cache_shape),
        grid=(N_TILES,),
        in_specs=[
            row(D_MODEL),
            _mod_spec(layer),
            _gain_spec(layer, 1),
            pl.BlockSpec(memory_space=pl.ANY),
            pl.BlockSpec(memory_space=pl.ANY),
            pl.BlockSpec(memory_space=pl.ANY),
        ],
        out_specs=(row(3 * ATTN_WIDTH), row(2 * SGU_WIDTH), row(FNET_WIDTH), cache_spec, cache_spec),
        input_output_aliases={4: 3, 5: 4},
        scratch_shapes=[
            pltpu.VMEM((D_MODEL, PROJ_COLS), BF16),
            pltpu.VMEM((STAGE_SLOTS, STAGE_ROWS, PROJ_COLS), F32),
            pltpu.SemaphoreType.DMA((STAGE_SLOTS,)),
            pltpu.VMEM((TM, D_MODEL), BF16),
            pltpu.VMEM((TM, ATTN_WIDTH), F32),
            pltpu.VMEM((TM, ATTN_WIDTH), F32),
        ],
        compiler_params=_params(1),
        name="inproj",
    )(h, mod, gains, w_in, kt, vt)


HEAD_PAIRS = N_HEADS // 2
PAIR_WIDTH = 2 * HEAD_DIM
PAIR_COLS = [slice(j * PAIR_WIDTH, (j + 1) * PAIR_WIDTH) for j in range(HEAD_PAIRS)]


def _first_of_pair(rows):
    return lax.broadcasted_iota(jnp.int32, (rows, PAIR_WIDTH), 1) < HEAD_DIM


def _pair_queries(q):
    first = _first_of_pair(q.shape[0])
    zero = jnp.zeros_like(q)
    return jnp.concatenate([jnp.where(first, q, zero), jnp.where(first, zero, q)], axis=0)


def _pair_outputs(o):
    m = o.shape[0] // 2
    return jnp.where(_first_of_pair(m), o[:m], o[m:])


def _ctx_attn_kernel(q_ref, kt_ref, v_ref, o_ref):
    def one_request(b, carry):
        rows = pl.ds(pl.multiple_of(b * SEQ, SEQ), SEQ)
        q = q_ref[rows, :]
        kt = kt_ref[b].astype(BF16)
        v = v_ref[rows, :]
        s = jnp.concatenate([_dot(_pair_queries(q[:, c]), kt[c, :]) for c in PAIR_COLS], axis=0)
        e = jnp.exp(s - jnp.max(s, axis=-1, keepdims=True))
        inv = 1.0 / jnp.sum(e, axis=-1, keepdims=True)
        p = e.astype(BF16)
        outs = []
        for j, c in enumerate(PAIR_COLS):
            pair = slice(2 * j * SEQ, 2 * (j + 1) * SEQ)
            outs.append(_pair_outputs(_dot(p[pair, :], v[:, c]) * inv[pair, :]))
        o_ref[rows, :] = jnp.concatenate(outs, axis=-1).astype(BF16)
        return carry

    lax.fori_loop(0, REQ_PER_TILE, one_request, 0, unroll=2)


def _ctx_attention(qkv, kt, layer):
    return pl.pallas_call(
        _ctx_attn_kernel,
        out_shape=jax.ShapeDtypeStruct((CTX_TOKENS, ATTN_WIDTH), BF16),
        grid=(CTX_TILES,),
        in_specs=[
            pl.BlockSpec((TM, ATTN_WIDTH), lambda i: (i, 0)),
            pl.BlockSpec((REQ_PER_TILE, None, ATTN_WIDTH, SEQ), lambda i: (i, layer, 0, 0)),
            pl.BlockSpec((TM, ATTN_WIDTH), lambda i: (i, 2)),
        ],
        out_specs=pl.BlockSpec((TM, ATTN_WIDTH), lambda i: (i, 0)),
        compiler_params=_params(1),
        name="ctx_attention",
    )(qkv, kt, qkv)


N_DROW = 2 * WIN_H - 1
N_DCOL = 2 * WIN_W - 1
N_DROW_PAIRS = N_DROW - 1
KEY_ROWS = WIN_H
KEYS_LOCAL = KEY_ROWS * GRID_W
RPB_LANES = 2 * GRID_W


def _build_bias_table(rpb_ref, tab_ref):
    first = _first_of_pair(GRID_W)
    for h in range(N_HEADS):
        for d in range(N_DROW_PAIRS):
            lo = jnp.broadcast_to(rpb_ref[h, d:d + 1, :], (GRID_W, RPB_LANES))
            hi = jnp.broadcast_to(rpb_ref[h, d + 1:d + 2, :], (GRID_W, RPB_LANES))
            lo = pltpu.roll(lo, RPB_LANES - (WIN_W - 1), 1, stride=1, stride_axis=0)
            hi = pltpu.roll(hi, GRID_W - (WIN_W - 1), 1, stride=1, stride_axis=0)
            tab_ref[h, d] = jnp.where(first, lo, hi)


def _natten_kernel(qb_ref, kb_ref, vb_ref, kc_ref, vc_ref, rpb_ref, o_ref, tab_ref):
    @pl.when(pl.program_id(0) == 0)
    def _():
        _build_bias_table(rpb_ref, tab_ref)

    kc = kc_ref[...].astype(BF16)
    vc = vc_ref[...].astype(BF16)
    qcol = lax.broadcasted_iota(jnp.int32, (N_HEADS * GRID_W, KEYS_LOCAL), 0) % GRID_W
    kcol = lax.broadcasted_iota(jnp.int32, (N_HEADS * GRID_W, KEYS_LOCAL), 1) % GRID_W
    start = jnp.clip(qcol - WIN_W // 2, 0, GRID_W - WIN_W)
    col_ok = (kcol >= start) & (kcol < start + WIN_W)

    def one_row(r, carry):
        r0 = jnp.clip(r - KEY_ROWS // 2, 0, GRID_ROWS - KEY_ROWS)
        d0 = r0 - r + (WIN_H - 1)
        qrows = pl.ds(pl.multiple_of(r * GRID_W, GRID_W), GRID_W)
        krows = pl.ds(pl.multiple_of(r0 * GRID_W, GRID_W), KEYS_LOCAL)
        q = qb_ref[qrows, :]
        k = jnp.concatenate([kb_ref[krows, :], kc], axis=0)
        v = jnp.concatenate([vb_ref[krows, :], vc], axis=0)
        s = jnp.concatenate([_dot_nt(_pair_queries(q[:, c]), k[:, c]) for c in PAIR_COLS], axis=0)
        bias = jnp.concatenate(
            [jnp.concatenate([tab_ref[h, d0 + 2 * i] for i in range(KEY_ROWS // 2)], axis=-1)
             for h in range(N_HEADS)], axis=0)
        s_loc = jnp.where(col_ok, s[:, :KEYS_LOCAL] + bias, NEG_INF)
        s_ctx = s[:, KEYS_LOCAL:]
        mx = jnp.maximum(jnp.max(s_loc, axis=-1, keepdims=True), jnp.max(s_ctx, axis=-1, keepdims=True))
        e_loc = jnp.exp(s_loc - mx)
        e_ctx = jnp.exp(s_ctx - mx)
        inv = 1.0 / (jnp.sum(e_loc, axis=-1, keepdims=True) + jnp.sum(e_ctx, axis=-1, keepdims=True))
        p = jnp.concatenate([e_loc, e_ctx], axis=-1).astype(BF16)
        outs = []
        for j, c in enumerate(PAIR_COLS):
            pair = slice(2 * j * GRID_W, 2 * (j + 1) * GRID_W)
            outs.append(_pair_outputs(_dot(p[pair, :], v[:, c]) * inv[pair, :]))
        o_ref[qrows, :] = jnp.concatenate(outs, axis=-1).astype(BF16)
        return carry

    lax.fori_loop(0, GRID_ROWS, one_row, 0, unroll=2)


def _natten(qkv, cache_k, cache_v, rpb_pad, layer):
    lat0 = CTX_TOKENS // DEC_SEQ
    spec = lambda c: pl.BlockSpec((DEC_SEQ, ATTN_WIDTH), lambda b, c=c: (lat0 + b, c))
    cache_spec = pl.BlockSpec((None, None, PAST_LEN, ATTN_WIDTH), lambda b: (b, layer, 0, 0))
    return pl.pallas_call(
        _natten_kernel,
        out_shape=jax.ShapeDtypeStruct((LAT_TOKENS, ATTN_WIDTH), BF16),
        grid=(DEC_BATCH,),
        in_specs=[spec(0), spec(1), spec(2), cache_spec, cache_spec,
                  pl.BlockSpec((None, N_HEADS, N_DROW, RPB_LANES), lambda b: (layer, 0, 0, 0))],
        out_specs=pl.BlockSpec((DEC_SEQ, ATTN_WIDTH), lambda b: (b, 0)),
        scratch_shapes=[pltpu.VMEM((N_HEADS, N_DROW_PAIRS, GRID_W, RPB_LANES), F32)],
        compiler_params=_params(1),
        name="natten",
    )(qkv, qkv, qkv, cache_k, cache_v, rpb_pad)


def _sgu_kernel(u_ref, v_ref, g_ref, w_ref, b_ref, o_ref):
    u = _gelu_tanh(u_ref[...])
    v = _gelu_tanh(v_ref[...])
    ms = jnp.mean(v * v, axis=-1, keepdims=True)
    vn = v * lax.rsqrt(ms + RMS_EPS) * g_ref[...]
    group = lax.broadcasted_iota(jnp.int32, (CHUNK, SGU_WIDTH), 1) // SGU_GROUP_DIM
    w = w_ref[...].astype(BF16)
    bias = b_ref[...]
    for n in range(TM // CHUNK):
        rows = slice(n * CHUNK, (n + 1) * CHUNK)
        x = vn[rows, :]
        stacked = jnp.concatenate(
            [jnp.where(group == g, x, 0.0) for g in range(SGU_GROUPS)], axis=0).astype(BF16)
        mixed = _dot(w, stacked) + bias
        o_ref[rows, :] = (u[rows, :] * mixed).astype(BF16)


def _sgu(uv, sgu_norm, w_cat, bias_tile, layer):
    return pl.pallas_call(
        _sgu_kernel,
        out_shape=jax.ShapeDtypeStruct((TOKENS, SGU_WIDTH), BF16),
        grid=(N_TILES,),
        in_specs=[
            pl.BlockSpec((TM, SGU_WIDTH), lambda i: (i, 0)),
            pl.BlockSpec((TM, SGU_WIDTH), lambda i: (i, 1)),
            pl.BlockSpec((None, 1, SGU_WIDTH), lambda i: (layer, 0, 0)),
            pl.BlockSpec((None, CHUNK, SGU_GROUPS * CHUNK), lambda i: (layer, 0, 0)),
            pl.BlockSpec((None, CHUNK, SGU_WIDTH), lambda i: (layer, 0, 0)),
        ],
        out_specs=pl.BlockSpec((TM, SGU_WIDTH), lambda i: (i, 0)),
        compiler_params=_params(1),
        name="sgu",
    )(uv, uv, sgu_norm, w_cat, bias_tile)


def _dft_tables(n):
    k = np.arange(n, dtype=np.int64)
    ang = 2.0 * np.pi * ((k[:, None] * k[None, :]) % n).astype(np.float64) / n
    return np.cos(ang), np.sin(ang)


def _channel_tables():
    c, s = _dft_tables(FNET_GROUP_DIM)
    eye = np.eye(FNET_GROUPS)
    return np.concatenate([np.kron(eye, c), np.kron(eye, s)], axis=1)


_CHAN_TABLE = _channel_tables().astype(np.float32)
_CTX_POS_TABLE = np.concatenate(_dft_tables(SEQ), axis=1).astype(np.float32)
_LAT_POS_TABLE = np.concatenate(_dft_tables(DEC_SEQ), axis=1).astype(np.float32)
FNET_CTX_SCALE = np.float32(1.0 / np.sqrt(SEQ * FNET_GROUP_DIM))
FNET_LAT_SCALE = np.float32(1.0 / np.sqrt(DEC_SEQ * FNET_GROUP_DIM))


def _fnet_ctx_kernel(f_ref, chan_ref, pos_ref, o_ref):
    y = _dot(f_ref[...], chan_ref[...].astype(BF16))
    pos = pos_ref[...].astype(BF16)
    for b in range(TM // SEQ):
        rows = slice(b * SEQ, (b + 1) * SEQ)
        yc = y[rows, :FNET_WIDTH].astype(BF16)
        ys = y[rows, FNET_WIDTH:].astype(BF16)
        out = _dot(pos[:, :SEQ], yc) - _dot(pos[:, SEQ:], ys)
        o_ref[rows, :] = (out * FNET_CTX_SCALE).astype(BF16)


def _fnet_ctx(f, chan, pos):
    whole = lambda i: (0, 0)
    return pl.pallas_call(
        _fnet_ctx_kernel,
        out_shape=jax.ShapeDtypeStruct((CTX_TOKENS, FNET_WIDTH), BF16),
        grid=(CTX_TILES,),
        in_specs=[
            pl.BlockSpec((TM, FNET_WIDTH), lambda i: (i, 0)),
            pl.BlockSpec((FNET_WIDTH, 2 * FNET_WIDTH), whole),
            pl.BlockSpec((SEQ, 2 * SEQ), whole),
        ],
        out_specs=pl.BlockSpec((TM, FNET_WIDTH), lambda i: (i, 0)),
        compiler_params=_params(1),
        name="fnet_ctx",
    )(f, chan, pos)


FNET_LAT_ROWS = 256


def _fnet_lat_kernel(f0_ref, f1_ref, chan_ref, pos_ref, o_ref, yc_ref, ys_ref):
    @pl.when(pl.program_id(0) == 0)
    def _():
        for b, f_ref in enumerate((f0_ref, f1_ref)):
            y = _dot(f_ref[...], chan_ref[...].astype(BF16))
            cols = slice(b * FNET_WIDTH, (b + 1) * FNET_WIDTH)
            yc_ref[:, cols] = y[:, :FNET_WIDTH].astype(BF16)
            ys_ref[:, cols] = y[:, FNET_WIDTH:].astype(BF16)

    out = (_dot(pos_ref[:, :DEC_SEQ].astype(BF16), yc_ref[...])
           - _dot(pos_ref[:, DEC_SEQ:].astype(BF16), ys_ref[...]))
    out = out * FNET_LAT_SCALE
    for b in range(DEC_BATCH):
        o_ref[b] = out[:, b * FNET_WIDTH:(b + 1) * FNET_WIDTH].astype(BF16)


def _fnet_lat(f, chan, pos):
    lat0 = CTX_TOKENS // DEC_SEQ
    return pl.pallas_call(
        _fnet_lat_kernel,
        out_shape=jax.ShapeDtypeStruct((DEC_BATCH, DEC_SEQ, FNET_WIDTH), BF16),
        grid=(DEC_SEQ // FNET_LAT_ROWS,),
        in_specs=[
            pl.BlockSpec((DEC_SEQ, FNET_WIDTH), lambda j: (lat0, 0)),
            pl.BlockSpec((DEC_SEQ, FNET_WIDTH), lambda j: (lat0 + 1, 0)),
            pl.BlockSpec((FNET_WIDTH, 2 * FNET_WIDTH), lambda j: (0, 0)),
            pl.BlockSpec((FNET_LAT_ROWS, 2 * DEC_SEQ), lambda j: (j, 0)),
        ],
        out_specs=pl.BlockSpec((DEC_BATCH, FNET_LAT_ROWS, FNET_WIDTH), lambda j: (0, j, 0)),
        scratch_shapes=[pltpu.VMEM((DEC_SEQ, DEC_BATCH * FNET_WIDTH), BF16)] * 2,
        compiler_params=_params(1),
        name="fnet_lat",
    )(f, f, chan, pos)


def _merge_kernel(h_ref, m_ref, g_ref, w_in_hbm, bg_ref, oac_ref, oal_ref, ob_ref, occ_ref, ocl_ref,
                  pa_hbm, pb_hbm, pc_hbm, wo_hbm, o_ref, wg_ref, pa_ref, pb_ref, pc_ref, wo_ref, stage_ref, sem_ref,
                  *, layer):
    @pl.when(pl.program_id(0) == 0)
    def _():
        jobs = (_row_chunks(w_in_hbm.at[layer, :, pl.ds(PROJ_COLS, GATE_COLS)], wg_ref, D_MODEL)
                + _row_chunks(pa_hbm.at[layer], pa_ref, ATTN_WIDTH)
                + _row_chunks(pb_hbm.at[layer], pb_ref, SGU_WIDTH)
                + _row_chunks(pc_hbm.at[layer], pc_ref, FNET_WIDTH)
                + _row_chunks(wo_hbm.at[layer], wo_ref, D_MODEL))
        _stage_weights(jobs, stage_ref, sem_ref)

    ctx = pl.program_id(0) < CTX_TILES
    x = h_ref[...]
    shift = m_ref[0, 3:4, :]
    scale = m_ref[0, 4:5, :]
    gate = m_ref[0, 5:6, :]
    z = _rms_modulate(x, g_ref[...], shift, scale).astype(BF16)
    oa = jnp.where(ctx, oac_ref[...], oal_ref[...])
    oc = jnp.where(ctx, occ_ref[...], ocl_ref[...])
    branches = ((oa, pa_ref), (ob_ref[...], pb_ref), (oc, pc_ref))
    mix = None
    for j, (o, p_ref) in enumerate(branches):
        cols = slice(j * D_MODEL, (j + 1) * D_MODEL)
        gj = _sigmoid(_dot(z, wg_ref[:, cols]) + bg_ref[j:j + 1, :])
        term = gj * _dot(o, p_ref[...])
        mix = term if mix is None else mix + term
    o_ref[...] = x + gate * _dot(mix.astype(BF16), wo_ref[...])


def _merge(h, mod, gains, w_in, b_gate, oa_ctx, oa_lat, ob, oc_ctx, oc_lat, p_attn, p_sgu, p_fnet, w_out, layer):
    lw = lambda i: (layer, 0, 0)
    ctx_tile = lambda i: (_ctx_tile(i), 0)
    lat_tile = lambda i: (_lat_tile(i), 0)
    hbm = pl.BlockSpec(memory_space=pl.ANY)
    return pl.pallas_call(
        functools.partial(_merge_kernel, layer=layer),
        out_shape=jax.ShapeDtypeStruct((TOKENS, D_MODEL), F32),
        grid=(N_TILES,),
        in_specs=[
            pl.BlockSpec((TM, D_MODEL), lambda i: (i, 0)),
            _mod_spec(layer),
            _gain_spec(layer, 1),
            hbm,
            pl.BlockSpec((None, N_BRANCH, D_MODEL), lw),
            pl.BlockSpec((TM, ATTN_WIDTH), ctx_tile),
            pl.BlockSpec((TM, ATTN_WIDTH), lat_tile),
            pl.BlockSpec((TM, SGU_WIDTH), lambda i: (i, 0)),
            pl.BlockSpec((TM, FNET_WIDTH), ctx_tile),
            pl.BlockSpec((TM, FNET_WIDTH), lat_tile),
            hbm, hbm, hbm, hbm,
        ],
        out_specs=pl.BlockSpec((TM, D_MODEL), lambda i: (i, 0)),
        scratch_shapes=[
            pltpu.VMEM((D_MODEL, GATE_COLS), BF16),
            pltpu.VMEM((ATTN_WIDTH, D_MODEL), BF16),
            pltpu.VMEM((SGU_WIDTH, D_MODEL), BF16),
            pltpu.VMEM((FNET_WIDTH, D_MODEL), BF16),
            pltpu.VMEM((D_MODEL, D_MODEL), BF16),
            pltpu.VMEM((STAGE_SLOTS, STAGE_ROWS, GATE_COLS), F32),
            pltpu.SemaphoreType.DMA((STAGE_SLOTS,)),
        ],
        compiler_params=_params(1),
        name="merge",
    )(h, mod, gains, w_in, b_gate, oa_ctx, oa_lat, ob, oc_ctx, oc_lat, p_attn, p_sgu, p_fnet, w_out)


def kernel(x_prompt, x_sample, cache_k, cache_v, c, c_ctx, w_mod, b_mod, g_norm, ffn_w_gate, ffn_w_up,
           ffn_w_down, w_in, b_gate, rpb, sgu_norm, sgu_w, sgu_b, p_attn, p_sgu, p_fnet, w_out, g_final):
    cond = jnp.concatenate([c_ctx[None, :], c, jnp.zeros((COND_ROWS - N_COND, D_MODEL), F32)], axis=0)
    mod = _modulation(cond, w_mod, b_mod).reshape(DEPTH, COND_ROWS, N_MOD, D_MODEL)
    rpb_pad = jnp.pad(rpb, ((0, 0), (0, 0), (0, 0), (0, RPB_LANES - N_DCOL)))
    cache_k = cache_k.reshape(DEC_BATCH, DEPTH, PAST_LEN, ATTN_WIDTH)
    cache_v = cache_v.reshape(DEC_BATCH, DEPTH, PAST_LEN, ATTN_WIDTH)
    gains = g_norm.reshape(DEPTH, 3, 1, D_MODEL)
    g_final2 = g_final.reshape(1, D_MODEL)

    sgu_gain = sgu_norm.reshape(DEPTH, 1, SGU_WIDTH)
    w_cat = jnp.transpose(sgu_w, (0, 2, 1, 3)).reshape(DEPTH, CHUNK, SGU_GROUPS * CHUNK)
    bias_tile = jnp.repeat(jnp.transpose(sgu_b, (0, 2, 1)), SGU_GROUP_DIM, axis=2)

    chan = jnp.asarray(_CHAN_TABLE)
    pos_ctx = jnp.asarray(_CTX_POS_TABLE)
    pos_lat = jnp.asarray(_LAT_POS_TABLE)

    kt = jnp.zeros((BATCH, DEPTH, ATTN_WIDTH, SEQ), F32)
    vt = jnp.zeros((BATCH, DEPTH, ATTN_WIDTH, SEQ), F32)

    xs = (x_prompt.reshape(CTX_TOKENS, D_MODEL), x_sample.reshape(LAT_TOKENS, D_MODEL))
    for l in range(DEPTH):
        h = _ffn(xs, mod, gains, ffn_w_gate, ffn_w_up, ffn_w_down, g_final2, l, 0, split_in=(l == 0))
        qkv, uv, f, kt, vt = _inproj(h, mod, gains, w_in, kt, vt, l)
        oa_ctx = _ctx_attention(qkv, kt, l)
        oa_lat = _natten(qkv, cache_k, cache_v, rpb_pad, l)
        ob = _sgu(uv, sgu_gain, w_cat, bias_tile, l)
        oc_ctx = _fnet_ctx(f, chan, pos_ctx)
        oc_lat = _fnet_lat(f, chan, pos_lat).reshape(LAT_TOKENS, FNET_WIDTH)
        h = _merge(h, mod, gains, w_in, b_gate, oa_ctx, oa_lat, ob, oc_ctx, oc_lat, p_attn, p_sgu, p_fnet, w_out, l)
        xs = _ffn((h,), mod, gains, ffn_w_gate, ffn_w_up, ffn_w_down, g_final2, l, 2, split_out=(l == DEPTH - 1))
        xs = xs if l == DEPTH - 1 else (xs,)

    y_prompt, y_sample = xs
    to_cache = lambda t: jnp.transpose(t.reshape(BATCH, DEPTH, N_HEADS, HEAD_DIM, SEQ), (0, 1, 4, 2, 3))
    return (y_prompt.reshape(BATCH, SEQ, D_MODEL), y_sample.reshape(DEC_BATCH, DEC_SEQ, D_MODEL),
            to_cache(kt), to_cache(vt))
```

```python
import functools

import numpy as np
import jax
import jax.numpy as jnp
from jax import lax
from jax.experimental import pallas as pl
from jax.experimental.pallas import tpu as pltpu

D_MODEL = 1024
BATCH = 32
SEQ = 256
DEPTH = 4
DEC_BATCH = 2
DEC_SEQ = 2048
PAST_LEN = 256
GRID_W = 64
GRID_ROWS = DEC_SEQ // GRID_W
WIN_H = 8
WIN_W = 16
HEAD_DIM = 64
ATTN_WIDTH = 512
N_HEADS = 8
SGU_WIDTH = 256
SGU_GROUPS = 4
SGU_GROUP_DIM = 64
CHUNK = 128
FNET_WIDTH = 256
FNET_GROUPS = 4
FNET_GROUP_DIM = 64
N_BRANCH = 3
D_FF = 2816
N_MOD = 9
RMS_EPS = 1e-6
NEG_INF = -1e30

CTX_TOKENS = BATCH * SEQ
LAT_TOKENS = DEC_BATCH * DEC_SEQ
TOKENS = CTX_TOKENS + LAT_TOKENS
N_COND = 1 + DEC_BATCH
COND_ROWS = 8
PROJ_COLS = 3 * ATTN_WIDTH + 2 * SGU_WIDTH + FNET_WIDTH
GATE_COLS = N_BRANCH * D_MODEL

TM = 1024
SPLIT_TM = 512
N_TILES = TOKENS // TM
CTX_TILES = CTX_TOKENS // TM
REQ_PER_TILE = TM // SEQ
FF_CHUNK = 256
VMEM_LIMIT = 56 * 1024 * 1024

BF16 = jnp.bfloat16
F32 = jnp.float32
ATTN_SCALE = np.float32(HEAD_DIM ** -0.5)


def _cond_row(i, tm=TM):
    ctx_tiles = CTX_TOKENS // tm
    return jnp.where(i < ctx_tiles, 0, 1 + (i - ctx_tiles) // (DEC_SEQ // tm))


def _ctx_tile(i, tm=TM):
    return jnp.minimum(i, CTX_TOKENS // tm - 1)


def _lat_tile(i, tm=TM):
    return jnp.maximum(i - CTX_TOKENS // tm, 0)


def _params(n_axes):
    return pltpu.CompilerParams(dimension_semantics=("arbitrary",) * n_axes,
                                vmem_limit_bytes=VMEM_LIMIT)


def _dot(a, b):
    return jnp.dot(a, b, preferred_element_type=F32)


def _dot_nt(a, b):
    return lax.dot_general(a, b, (((1,), (1,)), ((), ())), preferred_element_type=F32)


def _rms_modulate(x, g, shift, scale):
    ms = jnp.mean(x * x, axis=-1, keepdims=True)
    y = x * lax.rsqrt(ms + RMS_EPS) * g
    return y * (1.0 + scale) + shift


def _silu(x):
    return x * (1.0 / (1.0 + jnp.exp(-x)))


def _sigmoid(x):
    return 1.0 / (1.0 + jnp.exp(-x))


def _gelu_tanh(x):
    c = np.float32(np.sqrt(2.0 / np.pi))
    return x * (0.5 * (1.0 + jnp.tanh(c * (x + np.float32(0.044715) * (x * x * x)))))


MOD_COLS = N_MOD * D_MODEL
MOD_BLOCK = 2304


def _mod_kernel(c_ref, w_ref, b_ref, o_ref):
    a = _silu(c_ref[...]).astype(BF16)
    o_ref[0] = _dot(a, w_ref[0].astype(BF16)) + b_ref[0]


def _modulation(cond, w_mod, b_mod):
    return pl.pallas_call(
        _mod_kernel,
        out_shape=jax.ShapeDtypeStruct((DEPTH, COND_ROWS, MOD_COLS), F32),
        grid=(DEPTH, MOD_COLS // MOD_BLOCK),
        in_specs=[
            pl.BlockSpec((COND_ROWS, D_MODEL), lambda l, j: (0, 0)),
            pl.BlockSpec((1, D_MODEL, MOD_BLOCK), lambda l, j: (l, 0, j)),
            pl.BlockSpec((1, 1, MOD_BLOCK), lambda l, j: (l, 0, j)),
        ],
        out_specs=pl.BlockSpec((1, COND_ROWS, MOD_BLOCK), lambda l, j: (l, 0, j)),
        compiler_params=_params(2),
        name="modulation",
    )(cond, w_mod, b_mod.reshape(DEPTH, 1, MOD_COLS))


def _mod_spec(layer, tm=TM):
    return pl.BlockSpec((None, 1, N_MOD, D_MODEL), lambda i: (layer, _cond_row(i, tm), 0, 0))


def _gain_spec(layer, sub):
    return pl.BlockSpec((None, None, 1, D_MODEL), lambda i: (layer, sub, 0, 0))


STAGE_ROWS = 256
STAGE_SLOTS = 2


def _stage_weights(jobs, stage_ref, sem_ref):
    def copy(k):
        src = jobs[k][0]
        rows, cols = src.shape
        return pltpu.make_async_copy(src, stage_ref.at[k % STAGE_SLOTS, :rows, :cols], sem_ref.at[k % STAGE_SLOTS])

    copy(0).start()
    for k, (src, dst) in enumerate(jobs):
        if k + 1 < len(jobs):
            copy(k + 1).start()
        copy(k).wait()
        rows, cols = src.shape
        dst[...] = stage_ref[k % STAGE_SLOTS, :rows, :cols].astype(BF16)


def _row_chunks(src, dst, rows):
    return [(src.at[pl.ds(r, STAGE_ROWS), :], dst.at[pl.ds(r, STAGE_ROWS), :]) for r in range(0, rows, STAGE_ROWS)]


N_FF_CHUNKS = D_FF // FF_CHUNK
FFN_WEIGHTS = 3
FFN_STAGE_SLOTS = 3


def _ffn_kernel(*refs, mod_base, split_in, split_out, layer, ffn_sub, tm):
    n_in = 2 if split_in else 1
    x_refs, (m_ref, g_ref, wg_hbm, wu_hbm, wd_hbm, gf_ref) = refs[:n_in], refs[n_in:n_in + 6]
    n_out = 2 if split_out else 1
    o_refs = refs[n_in + 6:n_in + 6 + n_out]
    z_ref, acc_ref, wg_ref, wu_ref, wd_ref, stage_ref, stage_d_ref, sem_ref = refs[n_in + 6 + n_out:]
    first_tile = pl.program_id(0) == 0
    is_ctx = pl.program_id(0) < CTX_TOKENS // tm

    def chunk(j):
        z = z_ref[...]
        act = (_silu(_dot(z, wg_ref[j])) * _dot(z, wu_ref[j])).astype(BF16)
        return _dot(act, wd_ref[j])

    def copies(j, slot):
        cols = pl.ds(pl.multiple_of(j * FF_CHUNK, FF_CHUNK), FF_CHUNK)
        srcs = (wg_hbm.at[layer, ffn_sub, :, cols], wu_hbm.at[layer, ffn_sub, :, cols], wd_hbm.at[layer, ffn_sub, cols, :])
        dsts = (stage_ref.at[0, slot], stage_ref.at[1, slot], stage_d_ref.at[slot])
        return [pltpu.make_async_copy(src, dst, sem_ref.at[w, slot]) for w, (src, dst) in enumerate(zip(srcs, dsts))]

    def start(j, slot):
        for w, c in enumerate(copies(j, slot)):
            c.start(priority=w % 2)

    def start_first_chunks():
        for j in range(FFN_STAGE_SLOTS):
            start(j, j)

    def staged_chunks():
        acc_ref[...] = jnp.zeros_like(acc_ref)

        def staged_chunk(j, carry):
            slot = j % FFN_STAGE_SLOTS
            for c in copies(j, slot):
                c.wait()
            wg_ref[j] = stage_ref[0, slot].astype(BF16)
            wu_ref[j] = stage_ref[1, slot].astype(BF16)
            wd_ref[j] = stage_d_ref[slot].astype(BF16)

            @pl.when(j + FFN_STAGE_SLOTS < N_FF_CHUNKS)
            def _():
                start(j + FFN_STAGE_SLOTS, slot)

            acc_ref[...] += chunk(j)
            return carry

        lax.fori_loop(0, N_FF_CHUNKS, staged_chunk, 0)

    def unrolled_chunks():
        for j in range(N_FF_CHUNKS):
            if j == 0:
                acc_ref[...] = chunk(j)
            else:
                acc_ref[...] += chunk(j)

    def tile(run_chunks, before=lambda: None):
        before()
        x = jnp.where(is_ctx, x_refs[0][...], x_refs[1][...]) if split_in else x_refs[0][...]
        shift = m_ref[0, mod_base:mod_base + 1, :]
        scale = m_ref[0, mod_base + 1:mod_base + 2, :]
        gate = m_ref[0, mod_base + 2:mod_base + 3, :]
        z_ref[...] = _rms_modulate(x, g_ref[...], shift, scale).astype(BF16)
        run_chunks()
        y = x + (0.5 * gate) * acc_ref[...]
        if not split_out:
            o_refs[0][...] = y
        else:
            ms = jnp.mean(y * y, axis=-1, keepdims=True)
            y = y * lax.rsqrt(ms + RMS_EPS) * gf_ref[...]

            @pl.when(is_ctx)
            def _():
                o_refs[0][...] = y

            @pl.when(jnp.logical_not(is_ctx))
            def _():
                o_refs[1][...] = y

    @pl.when(first_tile)
    def _():
        tile(staged_chunks, before=start_first_chunks)

    @pl.when(jnp.logical_not(first_tile))
    def _():
        tile(unrolled_chunks)


def _ffn(xs, mod, gains, wg, wu, wd, g_final, layer, sub, *, split_in=False, split_out=False):
    tm = SPLIT_TM if (split_in or split_out) else TM
    kern = functools.partial(_ffn_kernel, mod_base=0 if sub == 0 else 6, split_in=split_in, split_out=split_out,
                             layer=layer, ffn_sub=0 if sub == 0 else 1, tm=tm)
    row = pl.BlockSpec((tm, D_MODEL), lambda i: (i, 0))
    split_rows = [pl.BlockSpec((tm, D_MODEL), lambda i: (_ctx_tile(i, tm), 0)),
                  pl.BlockSpec((tm, D_MODEL), lambda i: (_lat_tile(i, tm), 0))]
    hbm = pl.BlockSpec(memory_space=pl.ANY)
    if split_out:
        out_shape = (jax.ShapeDtypeStruct((CTX_TOKENS, D_MODEL), F32), jax.ShapeDtypeStruct((LAT_TOKENS, D_MODEL), F32))
        out_specs = tuple(split_rows)
    else:
        out_shape = jax.ShapeDtypeStruct((TOKENS, D_MODEL), F32)
        out_specs = row
    return pl.pallas_call(
        kern,
        out_shape=out_shape,
        grid=(TOKENS // tm,),
        in_specs=(split_rows if split_in else [row]) + [
            _mod_spec(layer, tm),
            _gain_spec(layer, sub),
            hbm, hbm, hbm,
            pl.BlockSpec((1, D_MODEL), lambda i: (0, 0)),
        ],
        out_specs=out_specs,
        scratch_shapes=[
            pltpu.VMEM((tm, D_MODEL), BF16),
            pltpu.VMEM((tm, D_MODEL), F32),
            pltpu.VMEM((N_FF_CHUNKS, D_MODEL, FF_CHUNK), BF16),
            pltpu.VMEM((N_FF_CHUNKS, D_MODEL, FF_CHUNK), BF16),
            pltpu.VMEM((N_FF_CHUNKS, FF_CHUNK, D_MODEL), BF16),
            pltpu.VMEM((2, FFN_STAGE_SLOTS, D_MODEL, FF_CHUNK), F32),
            pltpu.VMEM((FFN_STAGE_SLOTS, FF_CHUNK, D_MODEL), F32),
            pltpu.SemaphoreType.DMA((FFN_WEIGHTS, FFN_STAGE_SLOTS)),
        ],
        compiler_params=_params(1),
        name="ffn",
    )(*xs, mod, gains, wg, wu, wd, g_final)


def _inproj_kernel(h_ref, m_ref, g_ref, w_hbm, kt_in, vt_in, pa_ref, pb_ref, pf_ref, kt_ref, vt_ref,
                   w_ref, stage_ref, sem_ref, z_ref, kf_ref, vf_ref, *, layer):
    del kt_in, vt_in

    @pl.when(pl.program_id(0) == 0)
    def _():
        _stage_weights(_row_chunks(w_hbm.at[layer, :, pl.ds(0, PROJ_COLS)], w_ref, D_MODEL), stage_ref, sem_ref)

    is_ctx = pl.program_id(0) < CTX_TILES
    q_cols, k_cols, v_cols = (slice(c * ATTN_WIDTH, (c + 1) * ATTN_WIDTH) for c in range(3))
    rest_cols = slice(3 * ATTN_WIDTH, PROJ_COLS)

    def project(cols):
        return _dot(z_ref[...], w_ref[:, cols])

    def tile(write_cache):
        shift = m_ref[0, 3:4, :]
        scale = m_ref[0, 4:5, :]
        z_ref[...] = _rms_modulate(h_ref[...], g_ref[...], shift, scale).astype(BF16)
        for cols, full_ref, cache_ref in ((k_cols, kf_ref, kt_ref), (v_cols, vf_ref, vt_ref)):
            y = project(cols)
            pa_ref[:, cols] = y.astype(BF16)
            if write_cache:
                full_ref[...] = y
                for b in range(REQ_PER_TILE):
                    cache_ref[b] = full_ref[b * SEQ:(b + 1) * SEQ, :].T
        pa_ref[:, q_cols] = (project(q_cols) * ATTN_SCALE).astype(BF16)
        rest = project(rest_cols)
        pb_ref[...] = rest[:, :2 * SGU_WIDTH]
        pf_ref[...] = rest[:, 2 * SGU_WIDTH:].astype(BF16)

    @pl.when(is_ctx)
    def _():
        tile(True)

    @pl.when(jnp.logical_not(is_ctx))
    def _():
        tile(False)


def _inproj(h, mod, gains, w_in, kt, vt, layer):
    cache_shape = jax.ShapeDtypeStruct((BATCH, DEPTH, ATTN_WIDTH, SEQ), F32)
    cache_spec = pl.BlockSpec((REQ_PER_TILE, None, ATTN_WIDTH, SEQ), lambda i: (_ctx_tile(i), layer, 0, 0))
    row = lambda width: pl.BlockSpec((TM, width), lambda i: (i, 0))
    return pl.pallas_call(
        functools.partial(_inproj_kernel, layer=layer),
        out_shape=(jax.ShapeDtypeStruct((TOKENS, 3 * ATTN_WIDTH), BF16),
                   jax.ShapeDtypeStruct((TOKENS, 2 * SGU_WIDTH), F32),
                   jax.ShapeDtypeStruct((TOKENS, FNET_WIDTH), BF16), cache_shape, cache_shape),
        grid=(N_TILES,),
        in_specs=[
            row(D_MODEL),
            _mod_spec(layer),
            _gain_spec(layer, 1),
            pl.BlockSpec(memory_space=pl.ANY),
            pl.BlockSpec(memory_space=pl.ANY),
            pl.BlockSpec(memory_space=pl.ANY),
        ],
        out_specs=(row(3 * ATTN_WIDTH), row(2 * SGU_WIDTH), row(FNET_WIDTH), cache_spec, cache_spec),
        input_output_aliases={4: 3, 5: 4},
        scratch_shapes=[
            pltpu.VMEM((D_MODEL, PROJ_COLS), BF16),
            pltpu.VMEM((STAGE_SLOTS, STAGE_ROWS, PROJ_COLS), F32),
            pltpu.SemaphoreType.DMA((STAGE_SLOTS,)),
            pltpu.VMEM((TM, D_MODEL), BF16),
            pltpu.VMEM((TM, ATTN_WIDTH), F32),
            pltpu.VMEM((TM, ATTN_WIDTH), F32),
        ],
        compiler_params=_params(1),
        name="inproj",
    )(h, mod, gains, w_in, kt, vt)


HEAD_PAIRS = N_HEADS // 2
PAIR_WIDTH = 2 * HEAD_DIM
PAIR_COLS = [slice(j * PAIR_WIDTH, (j + 1) * PAIR_WIDTH) for j in range(HEAD_PAIRS)]


def _first_of_pair(rows):
    return lax.broadcasted_iota(jnp.int32, (rows, PAIR_WIDTH), 1) < HEAD_DIM


def _pair_queries(q):
    first = _first_of_pair(q.shape[0])
    zero = jnp.zeros_like(q)
    return jnp.concatenate([jnp.where(first, q, zero), jnp.where(first, zero, q)], axis=0)


def _pair_outputs(o):
    m = o.shape[0] // 2
    return jnp.where(_first_of_pair(m), o[:m], o[m:])


def _ctx_attn_kernel(q_ref, kt_ref, v_ref, o_ref):
    def one_request(b, carry):
        rows = pl.ds(pl.multiple_of(b * SEQ, SEQ), SEQ)
        q = q_ref[rows, :]
        kt = kt_ref[b].astype(BF16)
        v = v_ref[rows, :]
        s = jnp.concatenate([_dot(_pair_queries(q[:, c]), kt[c, :]) for c in PAIR_COLS], axis=0)
        e = jnp.exp(s - jnp.max(s, axis=-1, keepdims=True))
        inv = 1.0 / jnp.sum(e, axis=-1, keepdims=True)
        p = e.astype(BF16)
        outs = []
        for j, c in enumerate(PAIR_COLS):
            pair = slice(2 * j * SEQ, 2 * (j + 1) * SEQ)
            outs.append(_pair_outputs(_dot(p[pair, :], v[:, c]) * inv[pair, :]))
        o_ref[rows, :] = jnp.concatenate(outs, axis=-1).astype(BF16)
        return carry

    lax.fori_loop(0, REQ_PER_TILE, one_request, 0, unroll=2)


def _ctx_attention(qkv, kt, layer):
    return pl.pallas_call(
        _ctx_attn_kernel,
        out_shape=jax.ShapeDtypeStruct((CTX_TOKENS, ATTN_WIDTH), BF16),
        grid=(CTX_TILES,),
        in_specs=[
            pl.BlockSpec((TM, ATTN_WIDTH), lambda i: (i, 0)),
            pl.BlockSpec((REQ_PER_TILE, None, ATTN_WIDTH, SEQ), lambda i: (i, layer, 0, 0)),
            pl.BlockSpec((TM, ATTN_WIDTH), lambda i: (i, 2)),
        ],
        out_specs=pl.BlockSpec((TM, ATTN_WIDTH), lambda i: (i, 0)),
        compiler_params=_params(1),
        name="ctx_attention",
    )(qkv, kt, qkv)


N_DROW = 2 * WIN_H - 1
N_DCOL = 2 * WIN_W - 1
N_DROW_PAIRS = N_DROW - 1
KEY_ROWS = WIN_H
KEYS_LOCAL = KEY_ROWS * GRID_W
RPB_LANES = 2 * GRID_W


def _build_bias_table(rpb_ref, tab_ref):
    first = _first_of_pair(GRID_W)
    for h in range(N_HEADS):
        for d in range(N_DROW_PAIRS):
            lo = jnp.broadcast_to(rpb_ref[h, d:d + 1, :], (GRID_W, RPB_LANES))
            hi = jnp.broadcast_to(rpb_ref[h, d + 1:d + 2, :], (GRID_W, RPB_LANES))
            lo = pltpu.roll(lo, RPB_LANES - (WIN_W - 1), 1, stride=1, stride_axis=0)
            hi = pltpu.roll(hi, GRID_W - (WIN_W - 1), 1, stride=1, stride_axis=0)
            tab_ref[h, d] = jnp.where(first, lo, hi)


def _natten_kernel(qb_ref, kb_ref, vb_ref, kc_ref, vc_ref, rpb_ref, o_ref, tab_ref):
    @pl.when(pl.program_id(0) == 0)
    def _():
        _build_bias_table(rpb_ref, tab_ref)

    kc = kc_ref[...].astype(BF16)
    vc = vc_ref[...].astype(BF16)
    qcol = lax.broadcasted_iota(jnp.int32, (N_HEADS * GRID_W, KEYS_LOCAL), 0) % GRID_W
    kcol = lax.broadcasted_iota(jnp.int32, (N_HEADS * GRID_W, KEYS_LOCAL), 1) % GRID_W
    start = jnp.clip(qcol - WIN_W // 2, 0, GRID_W - WIN_W)
    col_ok = (kcol >= start) & (kcol < start + WIN_W)

    def one_row(r, carry):
        r0 = jnp.clip(r - KEY_ROWS // 2, 0, GRID_ROWS - KEY_ROWS)
        d0 = r0 - r + (WIN_H - 1)
        qrows = pl.ds(pl.multiple_of(r * GRID_W, GRID_W), GRID_W)
        krows = pl.ds(pl.multiple_of(r0 * GRID_W, GRID_W), KEYS_LOCAL)
        q = qb_ref[qrows, :]
        k = jnp.concatenate([kb_ref[krows, :], kc], axis=0)
        v = jnp.concatenate([vb_ref[krows, :], vc], axis=0)
        s = jnp.concatenate([_dot_nt(_pair_queries(q[:, c]), k[:, c]) for c in PAIR_COLS], axis=0)
        bias = jnp.concatenate(
            [jnp.concatenate([tab_ref[h, d0 + 2 * i] for i in range(KEY_ROWS // 2)], axis=-1)
             for h in range(N_HEADS)], axis=0)
        s_loc = jnp.where(col_ok, s[:, :KEYS_LOCAL] + bias, NEG_INF)
        s_ctx = s[:, KEYS_LOCAL:]
        mx = jnp.maximum(jnp.max(s_loc, axis=-1, keepdims=True), jnp.max(s_ctx, axis=-1, keepdims=True))
        e_loc = jnp.exp(s_loc - mx)
        e_ctx = jnp.exp(s_ctx - mx)
        inv = 1.0 / (jnp.sum(e_loc, axis=-1, keepdims=True) + jnp.sum(e_ctx, axis=-1, keepdims=True))
        p = jnp.concatenate([e_loc, e_ctx], axis=-1).astype(BF16)
        outs = []
        for j, c in enumerate(PAIR_COLS):
            pair = slice(2 * j * GRID_W, 2 * (j + 1) * GRID_W)
            outs.append(_pair_outputs(_dot(p[pair, :], v[:, c]) * inv[pair, :]))
        o_ref[qrows, :] = jnp.concatenate(outs, axis=-1).astype(BF16)
        return carry

    lax.fori_loop(0, GRID_ROWS, one_row, 0, unroll=2)


def _natten(qkv, cache_k, cache_v, rpb_pad, layer):
    lat0 = CTX_TOKENS // DEC_SEQ
    spec = lambda c: pl.BlockSpec((DEC_SEQ, ATTN_WIDTH), lambda b, c=c: (lat0 + b, c))
    cache_spec = pl.BlockSpec((None, None, PAST_LEN, ATTN_WIDTH), lambda b: (b, layer, 0, 0))
    return pl.pallas_call(
        _natten_kernel,
        out_shape=jax.ShapeDtypeStruct((LAT_TOKENS, ATTN_WIDTH), BF16),
        grid=(DEC_BATCH,),
        in_specs=[spec(0), spec(1), spec(2), cache_spec, cache_spec,
                  pl.BlockSpec((None, N_HEADS, N_DROW, RPB_LANES), lambda b: (layer, 0, 0, 0))],
        out_specs=pl.BlockSpec((DEC_SEQ, ATTN_WIDTH), lambda b: (b, 0)),
        scratch_shapes=[pltpu.VMEM((N_HEADS, N_DROW_PAIRS, GRID_W, RPB_LANES), F32)],
        compiler_params=_params(1),
        name="natten",
    )(qkv, qkv, qkv, cache_k, cache_v, rpb_pad)


def _sgu_kernel(u_ref, v_ref, g_ref, w_ref, b_ref, o_ref):
    u = _gelu_tanh(u_ref[...])
    v = _gelu_tanh(v_ref[...])
    ms = jnp.mean(v * v, axis=-1, keepdims=True)
    vn = v * lax.rsqrt(ms + RMS_EPS) * g_ref[...]
    group = lax.broadcasted_iota(jnp.int32, (CHUNK, SGU_WIDTH), 1) // SGU_GROUP_DIM
    w = w_ref[...].astype(BF16)
    bias = b_ref[...]
    for n in range(TM // CHUNK):
        rows = slice(n * CHUNK, (n + 1) * CHUNK)
        x = vn[rows, :]
        stacked = jnp.concatenate(
            [jnp.where(group == g, x, 0.0) for g in range(SGU_GROUPS)], axis=0).astype(BF16)
        mixed = _dot(w, stacked) + bias
        o_ref[rows, :] = (u[rows, :] * mixed).astype(BF16)


def _sgu(uv, sgu_norm, w_cat, bias_tile, layer):
    return pl.pallas_call(
        _sgu_kernel,
        out_shape=jax.ShapeDtypeStruct((TOKENS, SGU_WIDTH), BF16),
        grid=(N_TILES,),
        in_specs=[
            pl.BlockSpec((TM, SGU_WIDTH), lambda i: (i, 0)),
            pl.BlockSpec((TM, SGU_WIDTH), lambda i: (i, 1)),
            pl.BlockSpec((None, 1, SGU_WIDTH), lambda i: (layer, 0, 0)),
            pl.BlockSpec((None, CHUNK, SGU_GROUPS * CHUNK), lambda i: (layer, 0, 0)),
            pl.BlockSpec((None, CHUNK, SGU_WIDTH), lambda i: (layer, 0, 0)),
        ],
        out_specs=pl.BlockSpec((TM, SGU_WIDTH), lambda i: (i, 0)),
        compiler_params=_params(1),
        name="sgu",
    )(uv, uv, sgu_norm, w_cat, bias_tile)


def _dft_tables(n):
    k = np.arange(n, dtype=np.int64)
    ang = 2.0 * np.pi * ((k[:, None] * k[None, :]) % n).astype(np.float64) / n
    return np.cos(ang), np.sin(ang)


def _channel_tables():
    c, s = _dft_tables(FNET_GROUP_DIM)
    eye = np.eye(FNET_GROUPS)
    return np.concatenate([np.kron(eye, c), np.kron(eye, s)], axis=1)


_CHAN_TABLE = _channel_tables().astype(np.float32)
_CTX_POS_TABLE = np.concatenate(_dft_tables(SEQ), axis=1).astype(np.float32)
_LAT_POS_TABLE = np.concatenate(_dft_tables(DEC_SEQ), axis=1).astype(np.float32)
FNET_CTX_SCALE = np.float32(1.0 / np.sqrt(SEQ * FNET_GROUP_DIM))
FNET_LAT_SCALE = np.float32(1.0 / np.sqrt(DEC_SEQ * FNET_GROUP_DIM))


def _fnet_ctx_kernel(f_ref, chan_ref, pos_ref, o_ref):
    y = _dot(f_ref[...], chan_ref[...].astype(BF16))
    pos = pos_ref[...].astype(BF16)
    for b in range(TM // SEQ):
        rows = slice(b * SEQ, (b + 1) * SEQ)
        yc = y[rows, :FNET_WIDTH].astype(BF16)
        ys = y[rows, FNET_WIDTH:].astype(BF16)
        out = _dot(pos[:, :SEQ], yc) - _dot(pos[:, SEQ:], ys)
        o_ref[rows, :] = (out * FNET_CTX_SCALE).astype(BF16)


def _fnet_ctx(f, chan, pos):
    whole = lambda i: (0, 0)
    return pl.pallas_call(
        _fnet_ctx_kernel,
        out_shape=jax.ShapeDtypeStruct((CTX_TOKENS, FNET_WIDTH), BF16),
        grid=(CTX_TILES,),
        in_specs=[
            pl.BlockSpec((TM, FNET_WIDTH), lambda i: (i, 0)),
            pl.BlockSpec((FNET_WIDTH, 2 * FNET_WIDTH), whole),
            pl.BlockSpec((SEQ, 2 * SEQ), whole),
        ],
        out_specs=pl.BlockSpec((TM, FNET_WIDTH), lambda i: (i, 0)),
        compiler_params=_params(1),
        name="fnet_ctx",
    )(f, chan, pos)


FNET_LAT_ROWS = 256


def _fnet_lat_kernel(f0_ref, f1_ref, chan_ref, pos_ref, o_ref, yc_ref, ys_ref):
    @pl.when(pl.program_id(0) == 0)
    def _():
        for b, f_ref in enumerate((f0_ref, f1_ref)):
            y = _dot(f_ref[...], chan_ref[...].astype(BF16))
            cols = slice(b * FNET_WIDTH, (b + 1) * FNET_WIDTH)
            yc_ref[:, cols] = y[:, :FNET_WIDTH].astype(BF16)
            ys_ref[:, cols] = y[:, FNET_WIDTH:].astype(BF16)

    out = (_dot(pos_ref[:, :DEC_SEQ].astype(BF16), yc_ref[...])
           - _dot(pos_ref[:, DEC_SEQ:].astype(BF16), ys_ref[...]))
    out = out * FNET_LAT_SCALE
    for b in range(DEC_BATCH):
        o_ref[b] = out[:, b * FNET_WIDTH:(b + 1) * FNET_WIDTH].astype(BF16)


def _fnet_lat(f, chan, pos):
    lat0 = CTX_TOKENS // DEC_SEQ
    return pl.pallas_call(
        _fnet_lat_kernel,
        out_shape=jax.ShapeDtypeStruct((DEC_BATCH, DEC_SEQ, FNET_WIDTH), BF16),
        grid=(DEC_SEQ // FNET_LAT_ROWS,),
        in_specs=[
            pl.BlockSpec((DEC_SEQ, FNET_WIDTH), lambda j: (lat0, 0)),
            pl.BlockSpec((DEC_SEQ, FNET_WIDTH), lambda j: (lat0 + 1, 0)),
            pl.BlockSpec((FNET_WIDTH, 2 * FNET_WIDTH), lambda j: (0, 0)),
            pl.BlockSpec((FNET_LAT_ROWS, 2 * DEC_SEQ), lambda j: (j, 0)),
        ],
        out_specs=pl.BlockSpec((DEC_BATCH, FNET_LAT_ROWS, FNET_WIDTH), lambda j: (0, j, 0)),
        scratch_shapes=[pltpu.VMEM((DEC_SEQ, DEC_BATCH * FNET_WIDTH), BF16)] * 2,
        compiler_params=_params(1),
        name="fnet_lat",
    )(f, f, chan, pos)


def _merge_kernel(h_ref, m_ref, g_ref, w_in_hbm, bg_ref, oac_ref, oal_ref, ob_ref, occ_ref, ocl_ref,
                  pa_hbm, pb_hbm, pc_hbm, wo_hbm, o_ref, wg_ref, pa_ref, pb_ref, pc_ref, wo_ref, stage_ref, sem_ref,
                  *, layer):
    @pl.when(pl.program_id(0) == 0)
    def _():
        jobs = (_row_chunks(w_in_hbm.at[layer, :, pl.ds(PROJ_COLS, GATE_COLS)], wg_ref, D_MODEL)
                + _row_chunks(pa_hbm.at[layer], pa_ref, ATTN_WIDTH)
                + _row_chunks(pb_hbm.at[layer], pb_ref, SGU_WIDTH)
                + _row_chunks(pc_hbm.at[layer], pc_ref, FNET_WIDTH)
                + _row_chunks(wo_hbm.at[layer], wo_ref, D_MODEL))
        _stage_weights(jobs, stage_ref, sem_ref)

    ctx = pl.program_id(0) < CTX_TILES
    x = h_ref[...]
    shift = m_ref[0, 3:4, :]
    scale = m_ref[0, 4:5, :]
    gate = m_ref[0, 5:6, :]
    z = _rms_modulate(x, g_ref[...], shift, scale).astype(BF16)
    oa = jnp.where(ctx, oac_ref[...], oal_ref[...])
    oc = jnp.where(ctx, occ_ref[...], ocl_ref[...])
    branches = ((oa, pa_ref), (ob_ref[...], pb_ref), (oc, pc_ref))
    mix = None
    for j, (o, p_ref) in enumerate(branches):
        cols = slice(j * D_MODEL, (j + 1) * D_MODEL)
        gj = _sigmoid(_dot(z, wg_ref[:, cols]) + bg_ref[j:j + 1, :])
        term = gj * _dot(o, p_ref[...])
        mix = term if mix is None else mix + term
    o_ref[...] = x + gate * _dot(mix.astype(BF16), wo_ref[...])


def _merge(h, mod, gains, w_in, b_gate, oa_ctx, oa_lat, ob, oc_ctx, oc_lat, p_attn, p_sgu, p_fnet, w_out, layer):
    lw = lambda i: (layer, 0, 0)
    ctx_tile = lambda i: (_ctx_tile(i), 0)
    lat_tile = lambda i: (_lat_tile(i), 0)
    hbm = pl.BlockSpec(memory_space=pl.ANY)
    return pl.pallas_call(
        functools.partial(_merge_kernel, layer=layer),
        out_shape=jax.ShapeDtypeStruct((TOKENS, D_MODEL), F32),
        grid=(N_TILES,),
        in_specs=[
            pl.BlockSpec((TM, D_MODEL), lambda i: (i, 0)),
            _mod_spec(layer),
            _gain_spec(layer, 1),
            hbm,
            pl.BlockSpec((None, N_BRANCH, D_MODEL), lw),
            pl.BlockSpec((TM, ATTN_WIDTH), ctx_tile),
            pl.BlockSpec((TM, ATTN_WIDTH), lat_tile),
            pl.BlockSpec((TM, SGU_WIDTH), lambda i: (i, 0)),
            pl.BlockSpec((TM, FNET_WIDTH), ctx_tile),
            pl.BlockSpec((TM, FNET_WIDTH), lat_tile),
            hbm, hbm, hbm, hbm,
        ],
        out_specs=pl.BlockSpec((TM, D_MODEL), lambda i: (i, 0)),
        scratch_shapes=[
            pltpu.VMEM((D_MODEL, GATE_COLS), BF16),
            pltpu.VMEM((ATTN_WIDTH, D_MODEL), BF16),
            pltpu.VMEM((SGU_WIDTH, D_MODEL), BF16),
            pltpu.VMEM((FNET_WIDTH, D_MODEL), BF16),
            pltpu.VMEM((D_MODEL, D_MODEL), BF16),
            pltpu.VMEM((STAGE_SLOTS, STAGE_ROWS, GATE_COLS), F32),
            pltpu.SemaphoreType.DMA((STAGE_SLOTS,)),
        ],
        compiler_params=_params(1),
        name="merge",
    )(h, mod, gains, w_in, b_gate, oa_ctx, oa_lat, ob, oc_ctx, oc_lat, p_attn, p_sgu, p_fnet, w_out)


def kernel(x_prompt, x_sample, cache_k, cache_v, c, c_ctx, w_mod, b_mod, g_norm, ffn_w_gate, ffn_w_up,
           ffn_w_down, w_in, b_gate, rpb, sgu_norm, sgu_w, sgu_b, p_attn, p_sgu, p_fnet, w_out, g_final):
    cond = jnp.concatenate([c_ctx[None, :], c, jnp.zeros((COND_ROWS - N_COND, D_MODEL), F32)], axis=0)
    mod = _modulation(cond, w_mod, b_mod).reshape(DEPTH, COND_ROWS, N_MOD, D_MODEL)
    rpb_pad = jnp.pad(rpb, ((0, 0), (0, 0), (0, 0), (0, RPB_LANES - N_DCOL)))
    cache_k = cache_k.reshape(DEC_BATCH, DEPTH, PAST_LEN, ATTN_WIDTH)
    cache_v = cache_v.reshape(DEC_BATCH, DEPTH, PAST_LEN, ATTN_WIDTH)
    gains = g_norm.reshape(DEPTH, 3, 1, D_MODEL)
    g_final2 = g_final.reshape(1, D_MODEL)

    sgu_gain = sgu_norm.reshape(DEPTH, 1, SGU_WIDTH)
    w_cat = jnp.transpose(sgu_w, (0, 2, 1, 3)).reshape(DEPTH, CHUNK, SGU_GROUPS * CHUNK)
    bias_tile = jnp.repeat(jnp.transpose(sgu_b, (0, 2, 1)), SGU_GROUP_DIM, axis=2)

    chan = jnp.asarray(_CHAN_TABLE)
    pos_ctx = jnp.asarray(_CTX_POS_TABLE)
    pos_lat = jnp.asarray(_LAT_POS_TABLE)

    kt = jnp.zeros((BATCH, DEPTH, ATTN_WIDTH, SEQ), F32)
    vt = jnp.zeros((BATCH, DEPTH, ATTN_WIDTH, SEQ), F32)

    xs = (x_prompt.reshape(CTX_TOKENS, D_MODEL), x_sample.reshape(LAT_TOKENS, D_MODEL))
    for l in range(DEPTH):
        h = _ffn(xs, mod, gains, ffn_w_gate, ffn_w_up, ffn_w_down, g_final2, l, 0, split_in=(l == 0))
        qkv, uv, f, kt, vt = _inproj(h, mod, gains, w_in, kt, vt, l)
        oa_ctx = _ctx_attention(qkv, kt, l)
        oa_lat = _natten(qkv, cache_k, cache_v, rpb_pad, l)
        ob = _sgu(uv, sgu_gain, w_cat, bias_tile, l)
        oc_ctx = _fnet_ctx(f, chan, pos_ctx)
        oc_lat = _fnet_lat(f, chan, pos_lat).reshape(LAT_TOKENS, FNET_WIDTH)
        h = _merge(h, mod, gains, w_in, b_gate, oa_ctx, oa_lat, ob, oc_ctx, oc_lat, p_attn, p_sgu, p_fnet, w_out, l)
        xs = _ffn((h,), mod, gains, ffn_w_gate, ffn_w_up, ffn_w_down, g_final2, l, 2, split_out=(l == DEPTH - 1))
        xs = xs if l == DEPTH - 1 else (xs,)

    y_prompt, y_sample = xs
    to_cache = lambda t: jnp.transpose(t.reshape(BATCH, DEPTH, N_HEADS, HEAD_DIM, SEQ), (0, 1, 4, 2, 3))
    return (y_prompt.reshape(BATCH, SEQ, D_MODEL), y_sample.reshape(DEC_BATCH, DEC_SEQ, D_MODEL),
            to_cache(kt), to_cache(vt))
```

```python
import functools

import numpy as np
import jax
import jax.numpy as jnp
from jax import lax
from jax.experimental import pallas as pl
from jax.experimental.pallas import tpu as pltpu

D_MODEL = 1024
BATCH = 32
SEQ = 256
DEPTH = 4
DEC_BATCH = 2
DEC_SEQ = 2048
PAST_LEN = 256
GRID_W = 64
GRID_ROWS = DEC_SEQ // GRID_W
WIN_H = 8
WIN_W = 16
HEAD_DIM = 64
ATTN_WIDTH = 512
N_HEADS = 8
SGU_WIDTH = 256
SGU_GROUPS = 4
SGU_GROUP_DIM = 64
CHUNK = 128
FNET_WIDTH = 256
FNET_GROUPS = 4
FNET_GROUP_DIM = 64
N_BRANCH = 3
D_FF = 2816
N_MOD = 9
RMS_EPS = 1e-6
NEG_INF = -1e30

CTX_TOKENS = BATCH * SEQ
LAT_TOKENS = DEC_BATCH * DEC_SEQ
TOKENS = CTX_TOKENS + LAT_TOKENS
N_COND = 1 + DEC_BATCH
COND_ROWS = 8
PROJ_COLS = 3 * ATTN_WIDTH + 2 * SGU_WIDTH + FNET_WIDTH
GATE_COLS = N_BRANCH * D_MODEL

TM = 1024
SPLIT_TM = 512
N_TILES = TOKENS // TM
CTX_TILES = CTX_TOKENS // TM
REQ_PER_TILE = TM // SEQ
FF_CHUNK = 256
VMEM_LIMIT = 56 * 1024 * 1024

BF16 = jnp.bfloat16
F32 = jnp.float32
ATTN_SCALE = np.float32(HEAD_DIM ** -0.5)


def _cond_row(i, tm=TM):
    ctx_tiles = CTX_TOKENS // tm
    return jnp.where(i < ctx_tiles, 0, 1 + (i - ctx_tiles) // (DEC_SEQ // tm))


def _ctx_tile(i, tm=TM):
    return jnp.minimum(i, CTX_TOKENS // tm - 1)


def _lat_tile(i, tm=TM):
    return jnp.maximum(i - CTX_TOKENS // tm, 0)


def _params(n_axes):
    return pltpu.CompilerParams(dimension_semantics=("arbitrary",) * n_axes,
                                vmem_limit_bytes=VMEM_LIMIT)


def _dot(a, b):
    return jnp.dot(a, b, preferred_element_type=F32)


def _dot_nt(a, b):
    return lax.dot_general(a, b, (((1,), (1,)), ((), ())), preferred_element_type=F32)


def _rms_modulate(x, g, shift, scale):
    ms = jnp.mean(x * x, axis=-1, keepdims=True)
    y = x * lax.rsqrt(ms + RMS_EPS) * g
    return y * (1.0 + scale) + shift


def _silu(x):
    return x * (1.0 / (1.0 + jnp.exp(-x)))


def _sigmoid(x):
    return 1.0 / (1.0 + jnp.exp(-x))


def _gelu_tanh(x):
    c = np.float32(np.sqrt(2.0 / np.pi))
    return x * (0.5 * (1.0 + jnp.tanh(c * (x + np.float32(0.044715) * (x * x * x)))))


MOD_COLS = N_MOD * D_MODEL
MOD_BLOCK = 4608


def _mod_kernel(c_ref, w_ref, b_ref, o_ref):
    a = _silu(c_ref[...]).astype(BF16)
    o_ref[0] = _dot(a, w_ref[0].astype(BF16)) + b_ref[0]


def _modulation(cond, w_mod, b_mod):
    return pl.pallas_call(
        _mod_kernel,
        out_shape=jax.ShapeDtypeStruct((DEPTH, COND_ROWS, MOD_COLS), F32),
        grid=(DEPTH, MOD_COLS // MOD_BLOCK),
        in_specs=[
            pl.BlockSpec((COND_ROWS, D_MODEL), lambda l, j: (0, 0)),
            pl.BlockSpec((1, D_MODEL, MOD_BLOCK), lambda l, j: (l, 0, j)),
            pl.BlockSpec((1, 1, MOD_BLOCK), lambda l, j: (l, 0, j)),
        ],
        out_specs=pl.BlockSpec((1, COND_ROWS, MOD_BLOCK), lambda l, j: (l, 0, j)),
        compiler_params=_params(2),
        name="modulation",
    )(cond, w_mod, b_mod.reshape(DEPTH, 1, MOD_COLS))


def _mod_spec(layer, tm=TM):
    return pl.BlockSpec((None, 1, N_MOD, D_MODEL), lambda i: (layer, _cond_row(i, tm), 0, 0))


def _gain_spec(layer, sub):
    return pl.BlockSpec((None, None, 1, D_MODEL), lambda i: (layer, sub, 0, 0))


STAGE_ROWS = 256
STAGE_SLOTS = 2


def _stage_weights(jobs, stage_ref, sem_ref):
    def copy(k):
        src = jobs[k][0]
        rows, cols = src.shape
        return pltpu.make_async_copy(src, stage_ref.at[k % STAGE_SLOTS, :rows, :cols], sem_ref.at[k % STAGE_SLOTS])

    copy(0).start()
    for k, (src, dst) in enumerate(jobs):
        if k + 1 < len(jobs):
            copy(k + 1).start()
        copy(k).wait()
        rows, cols = src.shape
        dst[...] = stage_ref[k % STAGE_SLOTS, :rows, :cols].astype(BF16)


def _row_chunks(src, dst, rows):
    return [(src.at[pl.ds(r, STAGE_ROWS), :], dst.at[pl.ds(r, STAGE_ROWS), :]) for r in range(0, rows, STAGE_ROWS)]


N_FF_CHUNKS = D_FF // FF_CHUNK
FFN_WEIGHTS = 3


def _ffn_kernel(*refs, mod_base, split_in, split_out, layer, ffn_sub, tm):
    n_in = 2 if split_in else 1
    x_refs, (m_ref, g_ref, wg_hbm, wu_hbm, wd_hbm, gf_ref) = refs[:n_in], refs[n_in:n_in + 6]
    n_out = 2 if split_out else 1
    o_refs = refs[n_in + 6:n_in + 6 + n_out]
    z_ref, acc_ref, wg_ref, wu_ref, wd_ref, stage_ref, stage_d_ref, sem_ref = refs[n_in + 6 + n_out:]
    first_tile = pl.program_id(0) == 0
    is_ctx = pl.program_id(0) < CTX_TOKENS // tm

    def chunk(j):
        z = z_ref[...]
        act = (_silu(_dot(z, wg_ref[j])) * _dot(z, wu_ref[j])).astype(BF16)
        return _dot(act, wd_ref[j])

    def copies(j, slot):
        cols = pl.ds(pl.multiple_of(j * FF_CHUNK, FF_CHUNK), FF_CHUNK)
        srcs = (wg_hbm.at[layer, ffn_sub, :, cols], wu_hbm.at[layer, ffn_sub, :, cols], wd_hbm.at[layer, ffn_sub, cols, :])
        dsts = (stage_ref.at[0, slot], stage_ref.at[1, slot], stage_d_ref.at[slot])
        return [pltpu.make_async_copy(src, dst, sem_ref.at[w, slot]) for w, (src, dst) in enumerate(zip(srcs, dsts))]

    def staged_chunks():
        for j in range(STAGE_SLOTS):
            for c in copies(j, j):
                c.start()
        acc_ref[...] = jnp.zeros_like(acc_ref)

        def staged_chunk(j, carry):
            slot = j % STAGE_SLOTS
            for c in copies(j, slot):
                c.wait()
            wg_ref[j] = stage_ref[0, slot].astype(BF16)
            wu_ref[j] = stage_ref[1, slot].astype(BF16)
            wd_ref[j] = stage_d_ref[slot].astype(BF16)

            @pl.when(j + STAGE_SLOTS < N_FF_CHUNKS)
            def _():
                for c in copies(j + STAGE_SLOTS, slot):
                    c.start()

            acc_ref[...] += chunk(j)
            return carry

        lax.fori_loop(0, N_FF_CHUNKS, staged_chunk, 0)

    def unrolled_chunks():
        for j in range(N_FF_CHUNKS):
            if j == 0:
                acc_ref[...] = chunk(j)
            else:
                acc_ref[...] += chunk(j)

    def tile(run_chunks):
        x = jnp.where(is_ctx, x_refs[0][...], x_refs[1][...]) if split_in else x_refs[0][...]
        shift = m_ref[0, mod_base:mod_base + 1, :]
        scale = m_ref[0, mod_base + 1:mod_base + 2, :]
        gate = m_ref[0, mod_base + 2:mod_base + 3, :]
        z_ref[...] = _rms_modulate(x, g_ref[...], shift, scale).astype(BF16)
        run_chunks()
        y = x + (0.5 * gate) * acc_ref[...]
        if not split_out:
            o_refs[0][...] = y
        else:
            ms = jnp.mean(y * y, axis=-1, keepdims=True)
            y = y * lax.rsqrt(ms + RMS_EPS) * gf_ref[...]

            @pl.when(is_ctx)
            def _():
                o_refs[0][...] = y

            @pl.when(jnp.logical_not(is_ctx))
            def _():
                o_refs[1][...] = y

    @pl.when(first_tile)
    def _():
        tile(staged_chunks)

    @pl.when(jnp.logical_not(first_tile))
    def _():
        tile(unrolled_chunks)


def _ffn(xs, mod, gains, wg, wu, wd, g_final, layer, sub, *, split_in=False, split_out=False):
    tm = SPLIT_TM if (split_in or split_out) else TM
    kern = functools.partial(_ffn_kernel, mod_base=0 if sub == 0 else 6, split_in=split_in, split_out=split_out,
                             layer=layer, ffn_sub=0 if sub == 0 else 1, tm=tm)
    row = pl.BlockSpec((tm, D_MODEL), lambda i: (i, 0))
    split_rows = [pl.BlockSpec((tm, D_MODEL), lambda i: (_ctx_tile(i, tm), 0)),
                  pl.BlockSpec((tm, D_MODEL), lambda i: (_lat_tile(i, tm), 0))]
    hbm = pl.BlockSpec(memory_space=pl.ANY)
    if split_out:
        out_shape = (jax.ShapeDtypeStruct((CTX_TOKENS, D_MODEL), F32), jax.ShapeDtypeStruct((LAT_TOKENS, D_MODEL), F32))
        out_specs = tuple(split_rows)
    else:
        out_shape = jax.ShapeDtypeStruct((TOKENS, D_MODEL), F32)
        out_specs = row
    return pl.pallas_call(
        kern,
        out_shape=out_shape,
        grid=(TOKENS // tm,),
        in_specs=(split_rows if split_in else [row]) + [
            _mod_spec(layer, tm),
            _gain_spec(layer, sub),
            hbm, hbm, hbm,
            pl.BlockSpec((1, D_MODEL), lambda i: (0, 0)),
        ],
        out_specs=out_specs,
        scratch_shapes=[
            pltpu.VMEM((tm, D_MODEL), BF16),
            pltpu.VMEM((tm, D_MODEL), F32),
            pltpu.VMEM((N_FF_CHUNKS, D_MODEL, FF_CHUNK), BF16),
            pltpu.VMEM((N_FF_CHUNKS, D_MODEL, FF_CHUNK), BF16),
            pltpu.VMEM((N_FF_CHUNKS, FF_CHUNK, D_MODEL), BF16),
            pltpu.VMEM((2, STAGE_SLOTS, D_MODEL, FF_CHUNK), F32),
            pltpu.VMEM((STAGE_SLOTS, FF_CHUNK, D_MODEL), F32),
            pltpu.SemaphoreType.DMA((FFN_WEIGHTS, STAGE_SLOTS)),
        ],
        compiler_params=_params(1),
        name="ffn",
    )(*xs, mod, gains, wg, wu, wd, g_final)


def _inproj_kernel(h_ref, m_ref, g_ref, w_hbm, kt_in, vt_in, pa_ref, pb_ref, pf_ref, kt_ref, vt_ref,
                   w_ref, stage_ref, sem_ref, z_ref, kf_ref, vf_ref, *, layer):
    del kt_in, vt_in

    @pl.when(pl.program_id(0) == 0)
    def _():
        _stage_weights(_row_chunks(w_hbm.at[layer, :, pl.ds(0, PROJ_COLS)], w_ref, D_MODEL), stage_ref, sem_ref)

    is_ctx = pl.program_id(0) < CTX_TILES
    q_cols, k_cols, v_cols = (slice(c * ATTN_WIDTH, (c + 1) * ATTN_WIDTH) for c in range(3))
    rest_cols = slice(3 * ATTN_WIDTH, PROJ_COLS)

    def project(cols):
        return _dot(z_ref[...], w_ref[:, cols])

    def tile(write_cache):
        shift = m_ref[0, 3:4, :]
        scale = m_ref[0, 4:5, :]
        z_ref[...] = _rms_modulate(h_ref[...], g_ref[...], shift, scale).astype(BF16)
        for cols, full_ref, cache_ref in ((k_cols, kf_ref, kt_ref), (v_cols, vf_ref, vt_ref)):
            y = project(cols)
            pa_ref[:, cols] = y.astype(BF16)
            if write_cache:
                full_ref[...] = y
                for b in range(REQ_PER_TILE):
                    cache_ref[b] = full_ref[b * SEQ:(b + 1) * SEQ, :].T
        pa_ref[:, q_cols] = (project(q_cols) * ATTN_SCALE).astype(BF16)
        rest = project(rest_cols)
        pb_ref[...] = rest[:, :2 * SGU_WIDTH]
        pf_ref[...] = rest[:, 2 * SGU_WIDTH:].astype(BF16)

    @pl.when(is_ctx)
    def _():
        tile(True)

    @pl.when(jnp.logical_not(is_ctx))
    def _():
        tile(False)


def _inproj(h, mod, gains, w_in, kt, vt, layer):
    cache_shape = jax.ShapeDtypeStruct((BATCH, DEPTH, ATTN_WIDTH, SEQ), F32)
    cache_spec = pl.BlockSpec((REQ_PER_TILE, None, ATTN_WIDTH, SEQ), lambda i: (_ctx_tile(i), layer, 0, 0))
    row = lambda width: pl.BlockSpec((TM, width), lambda i: (i, 0))
    return pl.pallas_call(
        functools.partial(_inproj_kernel, layer=layer),
        out_shape=(jax.ShapeDtypeStruct((TOKENS, 3 * ATTN_WIDTH), BF16),
                   jax.ShapeDtypeStruct((TOKENS, 2 * SGU_WIDTH), F32),
                   jax.ShapeDtypeStruct((TOKENS, FNET_WIDTH), BF16), cache_shape, cache_shape),
        grid=(N_TILES,),
        in_specs=[
            row(D_MODEL),
            _mod_spec(layer),
            _gain_spec(layer, 1),
            pl.BlockSpec(memory_space=pl.ANY),
            pl.BlockSpec(memory_space=pl.ANY),
            pl.BlockSpec(memory_space=pl.ANY),
        ],
        out_specs=(row(3 * ATTN_WIDTH), row(2 * SGU_WIDTH), row(FNET_WIDTH), cache_spec, cache_spec),
        input_output_aliases={4: 3, 5: 4},
        scratch_shapes=[
            pltpu.VMEM((D_MODEL, PROJ_COLS), BF16),
            pltpu.VMEM((STAGE_SLOTS, STAGE_ROWS, PROJ_COLS), F32),
            pltpu.SemaphoreType.DMA((STAGE_SLOTS,)),
            pltpu.VMEM((TM, D_MODEL), BF16),
            pltpu.VMEM((TM, ATTN_WIDTH), F32),
            pltpu.VMEM((TM, ATTN_WIDTH), F32),
        ],
        compiler_params=_params(1),
        name="inproj",
    )(h, mod, gains, w_in, kt, vt)


HEAD_PAIRS = N_HEADS // 2
PAIR_WIDTH = 2 * HEAD_DIM
PAIR_COLS = [slice(j * PAIR_WIDTH, (j + 1) * PAIR_WIDTH) for j in range(HEAD_PAIRS)]


def _first_of_pair(rows):
    return lax.broadcasted_iota(jnp.int32, (rows, PAIR_WIDTH), 1) < HEAD_DIM


def _pair_queries(q):
    first = _first_of_pair(q.shape[0])
    zero = jnp.zeros_like(q)
    return jnp.concatenate([jnp.where(first, q, zero), jnp.where(first, zero, q)], axis=0)


def _pair_outputs(o):
    m = o.shape[0] // 2
    return jnp.where(_first_of_pair(m), o[:m], o[m:])


def _ctx_attn_kernel(q_ref, kt_ref, v_ref, o_ref):
    def one_request(b, carry):
        rows = pl.ds(pl.multiple_of(b * SEQ, SEQ), SEQ)
        q = q_ref[rows, :]
        kt = kt_ref[b].astype(BF16)
        v = v_ref[rows, :]
        s = jnp.concatenate([_dot(_pair_queries(q[:, c]), kt[c, :]) for c in PAIR_COLS], axis=0)
        e = jnp.exp(s - jnp.max(s, axis=-1, keepdims=True))
        inv = 1.0 / jnp.sum(e, axis=-1, keepdims=True)
        p = e.astype(BF16)
        outs = []
        for j, c in enumerate(PAIR_COLS):
            pair = slice(2 * j * SEQ, 2 * (j + 1) * SEQ)
            outs.append(_pair_outputs(_dot(p[pair, :], v[:, c]) * inv[pair, :]))
        o_ref[rows, :] = jnp.concatenate(outs, axis=-1).astype(BF16)
        return carry

    lax.fori_loop(0, REQ_PER_TILE, one_request, 0, unroll=True)


def _ctx_attention(qkv, kt, layer):
    return pl.pallas_call(
        _ctx_attn_kernel,
        out_shape=jax.ShapeDtypeStruct((CTX_TOKENS, ATTN_WIDTH), BF16),
        grid=(CTX_TILES,),
        in_specs=[
            pl.BlockSpec((TM, ATTN_WIDTH), lambda i: (i, 0)),
            pl.BlockSpec((REQ_PER_TILE, None, ATTN_WIDTH, SEQ), lambda i: (i, layer, 0, 0)),
            pl.BlockSpec((TM, ATTN_WIDTH), lambda i: (i, 2)),
        ],
        out_specs=pl.BlockSpec((TM, ATTN_WIDTH), lambda i: (i, 0)),
        compiler_params=_params(1),
        name="ctx_attention",
    )(qkv, kt, qkv)


N_DROW = 2 * WIN_H - 1
N_DCOL = 2 * WIN_W - 1
N_DROW_PAIRS = N_DROW - 1
KEY_ROWS = WIN_H
KEYS_LOCAL = KEY_ROWS * GRID_W
RPB_LANES = 2 * GRID_W


def _build_bias_table(rpb_ref, tab_ref):
    first = _first_of_pair(GRID_W)
    for h in range(N_HEADS):
        for d in range(N_DROW_PAIRS):
            lo = jnp.broadcast_to(rpb_ref[h, d:d + 1, :], (GRID_W, RPB_LANES))
            hi = jnp.broadcast_to(rpb_ref[h, d + 1:d + 2, :], (GRID_W, RPB_LANES))
            lo = pltpu.roll(lo, RPB_LANES - (WIN_W - 1), 1, stride=1, stride_axis=0)
            hi = pltpu.roll(hi, GRID_W - (WIN_W - 1), 1, stride=1, stride_axis=0)
            tab_ref[h, d] = jnp.where(first, lo, hi)


def _natten_kernel(qb_ref, kb_ref, vb_ref, kc_ref, vc_ref, rpb_ref, o_ref, tab_ref):
    @pl.when(pl.program_id(0) == 0)
    def _():
        _build_bias_table(rpb_ref, tab_ref)

    kc = kc_ref[...].astype(BF16)
    vc = vc_ref[...].astype(BF16)
    qcol = lax.broadcasted_iota(jnp.int32, (N_HEADS * GRID_W, KEYS_LOCAL), 0) % GRID_W
    kcol = lax.broadcasted_iota(jnp.int32, (N_HEADS * GRID_W, KEYS_LOCAL), 1) % GRID_W
    start = jnp.clip(qcol - WIN_W // 2, 0, GRID_W - WIN_W)
    col_ok = (kcol >= start) & (kcol < start + WIN_W)

    def one_row(r, carry):
        r0 = jnp.clip(r - KEY_ROWS // 2, 0, GRID_ROWS - KEY_ROWS)
        d0 = r0 - r + (WIN_H - 1)
        qrows = pl.ds(pl.multiple_of(r * GRID_W, GRID_W), GRID_W)
        krows = pl.ds(pl.multiple_of(r0 * GRID_W, GRID_W), KEYS_LOCAL)
        q = qb_ref[qrows, :]
        k = jnp.concatenate([kb_ref[krows, :], kc], axis=0)
        v = jnp.concatenate([vb_ref[krows, :], vc], axis=0)
        s = jnp.concatenate([_dot_nt(_pair_queries(q[:, c]), k[:, c]) for c in PAIR_COLS], axis=0)
        bias = jnp.concatenate(
            [jnp.concatenate([tab_ref[h, d0 + 2 * i] for i in range(KEY_ROWS // 2)], axis=-1)
             for h in range(N_HEADS)], axis=0)
        s_loc = jnp.where(col_ok, s[:, :KEYS_LOCAL] + bias, NEG_INF)
        s_ctx = s[:, KEYS_LOCAL:]
        mx = jnp.maximum(jnp.max(s_loc, axis=-1, keepdims=True), jnp.max(s_ctx, axis=-1, keepdims=True))
        e_loc = jnp.exp(s_loc - mx)
        e_ctx = jnp.exp(s_ctx - mx)
        inv = 1.0 / (jnp.sum(e_loc, axis=-1, keepdims=True) + jnp.sum(e_ctx, axis=-1, keepdims=True))
        p = jnp.concatenate([e_loc, e_ctx], axis=-1).astype(BF16)
        outs = []
        for j, c in enumerate(PAIR_COLS):
            pair = slice(2 * j * GRID_W, 2 * (j + 1) * GRID_W)
            outs.append(_pair_outputs(_dot(p[pair, :], v[:, c]) * inv[pair, :]))
        o_ref[qrows, :] = jnp.concatenate(outs, axis=-1).astype(BF16)
        return carry

    lax.fori_loop(0, GRID_ROWS, one_row, 0, unroll=4)


def _natten(qkv, cache_k, cache_v, rpb_pad, layer):
    lat0 = CTX_TOKENS // DEC_SEQ
    spec = lambda c: pl.BlockSpec((DEC_SEQ, ATTN_WIDTH), lambda b, c=c: (lat0 + b, c))
    cache_spec = pl.BlockSpec((None, None, PAST_LEN, ATTN_WIDTH), lambda b: (b, layer, 0, 0))
    return pl.pallas_call(
        _natten_kernel,
        out_shape=jax.ShapeDtypeStruct((LAT_TOKENS, ATTN_WIDTH), BF16),
        grid=(DEC_BATCH,),
        in_specs=[spec(0), spec(1), spec(2), cache_spec, cache_spec,
                  pl.BlockSpec((None, N_HEADS, N_DROW, RPB_LANES), lambda b: (layer, 0, 0, 0))],
        out_specs=pl.BlockSpec((DEC_SEQ, ATTN_WIDTH), lambda b: (b, 0)),
        scratch_shapes=[pltpu.VMEM((N_HEADS, N_DROW_PAIRS, GRID_W, RPB_LANES), F32)],
        compiler_params=_params(1),
        name="natten",
    )(qkv, qkv, qkv, cache_k, cache_v, rpb_pad)


def _sgu_kernel(u_ref, v_ref, g_ref, w_ref, b_ref, o_ref):
    u = _gelu_tanh(u_ref[...])
    v = _gelu_tanh(v_ref[...])
    ms = jnp.mean(v * v, axis=-1, keepdims=True)
    vn = v * lax.rsqrt(ms + RMS_EPS) * g_ref[...]
    group = lax.broadcasted_iota(jnp.int32, (CHUNK, SGU_WIDTH), 1) // SGU_GROUP_DIM
    w = w_ref[...].astype(BF16)
    bias = b_ref[...]
    for n in range(TM // CHUNK):
        rows = slice(n * CHUNK, (n + 1) * CHUNK)
        x = vn[rows, :]
        stacked = jnp.concatenate(
            [jnp.where(group == g, x, 0.0) for g in range(SGU_GROUPS)], axis=0).astype(BF16)
        mixed = _dot(w, stacked) + bias
        o_ref[rows, :] = (u[rows, :] * mixed).astype(BF16)


def _sgu(uv, sgu_norm, w_cat, bias_tile, layer):
    return pl.pallas_call(
        _sgu_kernel,
        out_shape=jax.ShapeDtypeStruct((TOKENS, SGU_WIDTH), BF16),
        grid=(N_TILES,),
        in_specs=[
            pl.BlockSpec((TM, SGU_WIDTH), lambda i: (i, 0)),
            pl.BlockSpec((TM, SGU_WIDTH), lambda i: (i, 1)),
            pl.BlockSpec((None, 1, SGU_WIDTH), lambda i: (layer, 0, 0)),
            pl.BlockSpec((None, CHUNK, SGU_GROUPS * CHUNK), lambda i: (layer, 0, 0)),
            pl.BlockSpec((None, CHUNK, SGU_WIDTH), lambda i: (layer, 0, 0)),
        ],
        out_specs=pl.BlockSpec((TM, SGU_WIDTH), lambda i: (i, 0)),
        compiler_params=_params(1),
        name="sgu",
    )(uv, uv, sgu_norm, w_cat, bias_tile)


def _dft_tables(n):
    k = np.arange(n, dtype=np.int64)
    ang = 2.0 * np.pi * ((k[:, None] * k[None, :]) % n).astype(np.float64) / n
    return np.cos(ang), np.sin(ang)


def _channel_tables():
    c, s = _dft_tables(FNET_GROUP_DIM)
    eye = np.eye(FNET_GROUPS)
    return np.concatenate([np.kron(eye, c), np.kron(eye, s)], axis=1)


_CHAN_TABLE = _channel_tables().astype(np.float32)
_CTX_POS_TABLE = np.concatenate(_dft_tables(SEQ), axis=1).astype(np.float32)
_LAT_POS_TABLE = np.concatenate(_dft_tables(DEC_SEQ), axis=1).astype(np.float32)
FNET_CTX_SCALE = np.float32(1.0 / np.sqrt(SEQ * FNET_GROUP_DIM))
FNET_LAT_SCALE = np.float32(1.0 / np.sqrt(DEC_SEQ * FNET_GROUP_DIM))


def _fnet_ctx_kernel(f_ref, chan_ref, pos_ref, o_ref):
    y = _dot(f_ref[...], chan_ref[...].astype(BF16))
    pos = pos_ref[...].astype(BF16)
    for b in range(TM // SEQ):
        rows = slice(b * SEQ, (b + 1) * SEQ)
        yc = y[rows, :FNET_WIDTH].astype(BF16)
        ys = y[rows, FNET_WIDTH:].astype(BF16)
        out = _dot(pos[:, :SEQ], yc) - _dot(pos[:, SEQ:], ys)
        o_ref[rows, :] = (out * FNET_CTX_SCALE).astype(BF16)


def _fnet_ctx(f, chan, pos):
    whole = lambda i: (0, 0)
    return pl.pallas_call(
        _fnet_ctx_kernel,
        out_shape=jax.ShapeDtypeStruct((CTX_TOKENS, FNET_WIDTH), BF16),
        grid=(CTX_TILES,),
        in_specs=[
            pl.BlockSpec((TM, FNET_WIDTH), lambda i: (i, 0)),
            pl.BlockSpec((FNET_WIDTH, 2 * FNET_WIDTH), whole),
            pl.BlockSpec((SEQ, 2 * SEQ), whole),
        ],
        out_specs=pl.BlockSpec((TM, FNET_WIDTH), lambda i: (i, 0)),
        compiler_params=_params(1),
        name="fnet_ctx",
    )(f, chan, pos)


FNET_LAT_ROWS = 256


def _fnet_lat_kernel(f0_ref, f1_ref, chan_ref, pos_ref, o_ref, yc_ref, ys_ref):
    @pl.when(pl.program_id(0) == 0)
    def _():
        for b, f_ref in enumerate((f0_ref, f1_ref)):
            y = _dot(f_ref[...], chan_ref[...].astype(BF16))
            cols = slice(b * FNET_WIDTH, (b + 1) * FNET_WIDTH)
            yc_ref[:, cols] = y[:, :FNET_WIDTH].astype(BF16)
            ys_ref[:, cols] = y[:, FNET_WIDTH:].astype(BF16)

    out = (_dot(pos_ref[:, :DEC_SEQ].astype(BF16), yc_ref[...])
           - _dot(pos_ref[:, DEC_SEQ:].astype(BF16), ys_ref[...]))
    out = out * FNET_LAT_SCALE
    for b in range(DEC_BATCH):
        o_ref[b] = out[:, b * FNET_WIDTH:(b + 1) * FNET_WIDTH].astype(BF16)


def _fnet_lat(f, chan, pos):
    lat0 = CTX_TOKENS // DEC_SEQ
    return pl.pallas_call(
        _fnet_lat_kernel,
        out_shape=jax.ShapeDtypeStruct((DEC_BATCH, DEC_SEQ, FNET_WIDTH), BF16),
        grid=(DEC_SEQ // FNET_LAT_ROWS,),
        in_specs=[
            pl.BlockSpec((DEC_SEQ, FNET_WIDTH), lambda j: (lat0, 0)),
            pl.BlockSpec((DEC_SEQ, FNET_WIDTH), lambda j: (lat0 + 1, 0)),
            pl.BlockSpec((FNET_WIDTH, 2 * FNET_WIDTH), lambda j: (0, 0)),
            pl.BlockSpec((FNET_LAT_ROWS, 2 * DEC_SEQ), lambda j: (j, 0)),
        ],
        out_specs=pl.BlockSpec((DEC_BATCH, FNET_LAT_ROWS, FNET_WIDTH), lambda j: (0, j, 0)),
        scratch_shapes=[pltpu.VMEM((DEC_SEQ, DEC_BATCH * FNET_WIDTH), BF16)] * 2,
        compiler_params=_params(1),
        name="fnet_lat",
    )(f, f, chan, pos)


def _merge_kernel(h_ref, m_ref, g_ref, w_in_hbm, bg_ref, oac_ref, oal_ref, ob_ref, occ_ref, ocl_ref,
                  pa_hbm, pb_hbm, pc_hbm, wo_hbm, o_ref, wg_ref, pa_ref, pb_ref, pc_ref, wo_ref, stage_ref, sem_ref,
                  *, layer):
    @pl.when(pl.program_id(0) == 0)
    def _():
        jobs = (_row_chunks(w_in_hbm.at[layer, :, pl.ds(PROJ_COLS, GATE_COLS)], wg_ref, D_MODEL)
                + _row_chunks(pa_hbm.at[layer], pa_ref, ATTN_WIDTH)
                + _row_chunks(pb_hbm.at[layer], pb_ref, SGU_WIDTH)
                + _row_chunks(pc_hbm.at[layer], pc_ref, FNET_WIDTH)
                + _row_chunks(wo_hbm.at[layer], wo_ref, D_MODEL))
        _stage_weights(jobs, stage_ref, sem_ref)

    ctx = pl.program_id(0) < CTX_TILES
    x = h_ref[...]
    shift = m_ref[0, 3:4, :]
    scale = m_ref[0, 4:5, :]
    gate = m_ref[0, 5:6, :]
    z = _rms_modulate(x, g_ref[...], shift, scale).astype(BF16)
    oa = jnp.where(ctx, oac_ref[...], oal_ref[...])
    oc = jnp.where(ctx, occ_ref[...], ocl_ref[...])
    branches = ((oa, pa_ref), (ob_ref[...], pb_ref), (oc, pc_ref))
    mix = None
    for j, (o, p_ref) in enumerate(branches):
        cols = slice(j * D_MODEL, (j + 1) * D_MODEL)
        gj = _sigmoid(_dot(z, wg_ref[:, cols]) + bg_ref[j:j + 1, :])
        term = gj * _dot(o, p_ref[...])
        mix = term if mix is None else mix + term
    o_ref[...] = x + gate * _dot(mix.astype(BF16), wo_ref[...])


def _merge(h, mod, gains, w_in, b_gate, oa_ctx, oa_lat, ob, oc_ctx, oc_lat, p_attn, p_sgu, p_fnet, w_out, layer):
    lw = lambda i: (layer, 0, 0)
    ctx_tile = lambda i: (_ctx_tile(i), 0)
    lat_tile = lambda i: (_lat_tile(i), 0)
    hbm = pl.BlockSpec(memory_space=pl.ANY)
    return pl.pallas_call(
        functools.partial(_merge_kernel, layer=layer),
        out_shape=jax.ShapeDtypeStruct((TOKENS, D_MODEL), F32),
        grid=(N_TILES,),
        in_specs=[
            pl.BlockSpec((TM, D_MODEL), lambda i: (i, 0)),
            _mod_spec(layer),
            _gain_spec(layer, 1),
            hbm,
            pl.BlockSpec((None, N_BRANCH, D_MODEL), lw),
            pl.BlockSpec((TM, ATTN_WIDTH), ctx_tile),
            pl.BlockSpec((TM, ATTN_WIDTH), lat_tile),
            pl.BlockSpec((TM, SGU_WIDTH), lambda i: (i, 0)),
            pl.BlockSpec((TM, FNET_WIDTH), ctx_tile),
            pl.BlockSpec((TM, FNET_WIDTH), lat_tile),
            hbm, hbm, hbm, hbm,
        ],
        out_specs=pl.BlockSpec((TM, D_MODEL), lambda i: (i, 0)),
        scratch_shapes=[
            pltpu.VMEM((D_MODEL, GATE_COLS), BF16),
            pltpu.VMEM((ATTN_WIDTH, D_MODEL), BF16),
            pltpu.VMEM((SGU_WIDTH, D_MODEL), BF16),
            pltpu.VMEM((FNET_WIDTH, D_MODEL), BF16),
            pltpu.VMEM((D_MODEL, D_MODEL), BF16),
            pltpu.VMEM((STAGE_SLOTS, STAGE_ROWS, GATE_COLS), F32),
            pltpu.SemaphoreType.DMA((STAGE_SLOTS,)),
        ],
        compiler_params=_params(1),
        name="merge",
    )(h, mod, gains, w_in, b_gate, oa_ctx, oa_lat, ob, oc_ctx, oc_lat, p_attn, p_sgu, p_fnet, w_out)


def kernel(x_prompt, x_sample, cache_k, cache_v, c, c_ctx, w_mod, b_mod, g_norm, ffn_w_gate, ffn_w_up,
           ffn_w_down, w_in, b_gate, rpb, sgu_norm, sgu_w, sgu_b, p_attn, p_sgu, p_fnet, w_out, g_final):
    cond = jnp.concatenate([c_ctx[None, :], c, jnp.zeros((COND_ROWS - N_COND, D_MODEL), F32)], axis=0)
    mod = _modulation(cond, w_mod, b_mod).reshape(DEPTH, COND_ROWS, N_MOD, D_MODEL)
    rpb_pad = jnp.pad(rpb, ((0, 0), (0, 0), (0, 0), (0, RPB_LANES - N_DCOL)))
    cache_k = cache_k.reshape(DEC_BATCH, DEPTH, PAST_LEN, ATTN_WIDTH)
    cache_v = cache_v.reshape(DEC_BATCH, DEPTH, PAST_LEN, ATTN_WIDTH)
    gains = g_norm.reshape(DEPTH, 3, 1, D_MODEL)
    g_final2 = g_final.reshape(1, D_MODEL)

    sgu_gain = sgu_norm.reshape(DEPTH, 1, SGU_WIDTH)
    w_cat = jnp.transpose(sgu_w, (0, 2, 1, 3)).reshape(DEPTH, CHUNK, SGU_GROUPS * CHUNK)
    bias_tile = jnp.repeat(jnp.transpose(sgu_b, (0, 2, 1)), SGU_GROUP_DIM, axis=2)

    chan = jnp.asarray(_CHAN_TABLE)
    pos_ctx = jnp.asarray(_CTX_POS_TABLE)
    pos_lat = jnp.asarray(_LAT_POS_TABLE)

    kt = jnp.zeros((BATCH, DEPTH, ATTN_WIDTH, SEQ), F32)
    vt = jnp.zeros((BATCH, DEPTH, ATTN_WIDTH, SEQ), F32)

    xs = (x_prompt.reshape(CTX_TOKENS, D_MODEL), x_sample.reshape(LAT_TOKENS, D_MODEL))
    for l in range(DEPTH):
        h = _ffn(xs, mod, gains, ffn_w_gate, ffn_w_up, ffn_w_down, g_final2, l, 0, split_in=(l == 0))
        qkv, uv, f, kt, vt = _inproj(h, mod, gains, w_in, kt, vt, l)
        oa_ctx = _ctx_attention(qkv, kt, l)
        oa_lat = _natten(qkv, cache_k, cache_v, rpb_pad, l)
        ob = _sgu(uv, sgu_gain, w_cat, bias_tile, l)
        oc_ctx = _fnet_ctx(f, chan, pos_ctx)
        oc_lat = _fnet_lat(f, chan, pos_lat).reshape(LAT_TOKENS, FNET_WIDTH)
        h = _merge(h, mod, gains, w_in, b_gate, oa_ctx, oa_lat, ob, oc_ctx, oc_lat, p_attn, p_sgu, p_fnet, w_out, l)
        xs = _ffn((h,), mod, gains, ffn_w_gate, ffn_w_up, ffn_w_down, g_final2, l, 2, split_out=(l == DEPTH - 1))
        xs = xs if l == DEPTH - 1 else (xs,)

    y_prompt, y_sample = xs
    to_cache = lambda t: jnp.transpose(t.reshape(BATCH, DEPTH, N_HEADS, HEAD_DIM, SEQ), (0, 1, 4, 2, 3))
    return (y_prompt.reshape(BATCH, SEQ, D_MODEL), y_sample.reshape(DEC_BATCH, DEC_SEQ, D_MODEL),
            to_cache(kt), to_cache(vt))
```

```python
import functools

import numpy as np
import jax
import jax.numpy as jnp
from jax import lax
from jax.experimental import pallas as pl
from jax.experimental.pallas import tpu as pltpu

D_MODEL = 1024
BATCH = 32
SEQ = 256
DEPTH = 4
DEC_BATCH = 2
DEC_SEQ = 2048
PAST_LEN = 256
GRID_W = 64
GRID_ROWS = DEC_SEQ // GRID_W
WIN_H = 8
WIN_W = 16
HEAD_DIM = 64
ATTN_WIDTH = 512
N_HEADS = 8
SGU_WIDTH = 256
SGU_GROUPS = 4
SGU_GROUP_DIM = 64
CHUNK = 128
FNET_WIDTH = 256
FNET_GROUPS = 4
FNET_GROUP_DIM = 64
N_BRANCH = 3
D_FF = 2816
N_MOD = 9
RMS_EPS = 1e-6
NEG_INF = -1e30

CTX_TOKENS = BATCH * SEQ
LAT_TOKENS = DEC_BATCH * DEC_SEQ
TOKENS = CTX_TOKENS + LAT_TOKENS
N_COND = 1 + DEC_BATCH
COND_ROWS = 8
PROJ_COLS = 3 * ATTN_WIDTH + 2 * SGU_WIDTH + FNET_WIDTH
GATE_COLS = N_BRANCH * D_MODEL

TM = 1024
SPLIT_TM = 512
N_TILES = TOKENS // TM
CTX_TILES = CTX_TOKENS // TM
REQ_PER_TILE = TM // SEQ
FF_CHUNK = 256
VMEM_LIMIT = 56 * 1024 * 1024

BF16 = jnp.bfloat16
F32 = jnp.float32
ATTN_SCALE = np.float32(HEAD_DIM ** -0.5)


def _cond_row(i, tm=TM):
    ctx_tiles = CTX_TOKENS // tm
    return jnp.where(i < ctx_tiles, 0, 1 + (i - ctx_tiles) // (DEC_SEQ // tm))


def _ctx_tile(i, tm=TM):
    return jnp.minimum(i, CTX_TOKENS // tm - 1)


def _lat_tile(i, tm=TM):
    return jnp.maximum(i - CTX_TOKENS // tm, 0)


def _params(n_axes):
    return pltpu.CompilerParams(dimension_semantics=("arbitrary",) * n_axes,
                                vmem_limit_bytes=VMEM_LIMIT)


def _dot(a, b):
    return jnp.dot(a, b, preferred_element_type=F32)


def _dot_nt(a, b):
    return lax.dot_general(a, b, (((1,), (1,)), ((), ())), preferred_element_type=F32)


def _rms_modulate(x, g, shift, scale):
    ms = jnp.mean(x * x, axis=-1, keepdims=True)
    y = x * lax.rsqrt(ms + RMS_EPS) * g
    return y * (1.0 + scale) + shift


def _silu(x):
    return x * (1.0 / (1.0 + jnp.exp(-x)))


def _sigmoid(x):
    return 1.0 / (1.0 + jnp.exp(-x))


def _gelu_tanh(x):
    c = np.float32(np.sqrt(2.0 / np.pi))
    return x * (0.5 * (1.0 + jnp.tanh(c * (x + np.float32(0.044715) * (x * x * x)))))


MOD_COLS = N_MOD * D_MODEL
MOD_BLOCK = 4608


def _mod_kernel(c_ref, w_ref, b_ref, o_ref):
    a = _silu(c_ref[...]).astype(BF16)
    o_ref[0] = _dot(a, w_ref[0].astype(BF16)) + b_ref[0]


def _modulation(cond, w_mod, b_mod):
    return pl.pallas_call(
        _mod_kernel,
        out_shape=jax.ShapeDtypeStruct((DEPTH, COND_ROWS, MOD_COLS), F32),
        grid=(DEPTH, MOD_COLS // MOD_BLOCK),
        in_specs=[
            pl.BlockSpec((COND_ROWS, D_MODEL), lambda l, j: (0, 0)),
            pl.BlockSpec((1, D_MODEL, MOD_BLOCK), lambda l, j: (l, 0, j)),
            pl.BlockSpec((1, 1, MOD_BLOCK), lambda l, j: (l, 0, j)),
        ],
        out_specs=pl.BlockSpec((1, COND_ROWS, MOD_BLOCK), lambda l, j: (l, 0, j)),
        compiler_params=_params(2),
        name="modulation",
    )(cond, w_mod, b_mod.reshape(DEPTH, 1, MOD_COLS))


def _mod_spec(layer, tm=TM):
    return pl.BlockSpec((None, 1, N_MOD, D_MODEL), lambda i: (layer, _cond_row(i, tm), 0, 0))


def _gain_spec(layer, sub):
    return pl.BlockSpec((None, None, 1, D_MODEL), lambda i: (layer, sub, 0, 0))


STAGE_ROWS = 256
STAGE_SLOTS = 2


def _stage_weights(jobs, stage_ref, sem_ref):
    def copy(k):
        src = jobs[k][0]
        rows, cols = src.shape
        return pltpu.make_async_copy(src, stage_ref.at[k % STAGE_SLOTS, :rows, :cols], sem_ref.at[k % STAGE_SLOTS])

    copy(0).start()
    for k, (src, dst) in enumerate(jobs):
        if k + 1 < len(jobs):
            copy(k + 1).start()
        copy(k).wait()
        rows, cols = src.shape
        dst[...] = stage_ref[k % STAGE_SLOTS, :rows, :cols].astype(BF16)


def _row_chunks(src, dst, rows):
    return [(src.at[pl.ds(r, STAGE_ROWS), :], dst.at[pl.ds(r, STAGE_ROWS), :]) for r in range(0, rows, STAGE_ROWS)]


N_FF_CHUNKS = D_FF // FF_CHUNK
FFN_WEIGHTS = 3


def _ffn_kernel(*refs, mod_base, split_in, split_out, layer, ffn_sub, tm):
    n_in = 2 if split_in else 1
    x_refs, (m_ref, g_ref, wg_hbm, wu_hbm, wd_hbm, gf_ref) = refs[:n_in], refs[n_in:n_in + 6]
    n_out = 2 if split_out else 1
    o_refs = refs[n_in + 6:n_in + 6 + n_out]
    z_ref, acc_ref, wg_ref, wu_ref, wd_ref, stage_ref, stage_d_ref, sem_ref = refs[n_in + 6 + n_out:]
    first_tile = pl.program_id(0) == 0
    is_ctx = pl.program_id(0) < CTX_TOKENS // tm

    def chunk(j):
        z = z_ref[...]
        act = (_silu(_dot(z, wg_ref[j])) * _dot(z, wu_ref[j])).astype(BF16)
        return _dot(act, wd_ref[j])

    def copies(j, slot):
        cols = pl.ds(pl.multiple_of(j * FF_CHUNK, FF_CHUNK), FF_CHUNK)
        srcs = (wg_hbm.at[layer, ffn_sub, :, cols], wu_hbm.at[layer, ffn_sub, :, cols], wd_hbm.at[layer, ffn_sub, cols, :])
        dsts = (stage_ref.at[0, slot], stage_ref.at[1, slot], stage_d_ref.at[slot])
        return [pltpu.make_async_copy(src, dst, sem_ref.at[w, slot]) for w, (src, dst) in enumerate(zip(srcs, dsts))]

    def staged_chunks():
        for j in range(STAGE_SLOTS):
            for c in copies(j, j):
                c.start()
        acc_ref[...] = jnp.zeros_like(acc_ref)

        def staged_chunk(j, carry):
            slot = j % STAGE_SLOTS
            for c in copies(j, slot):
                c.wait()
            wg_ref[j] = stage_ref[0, slot].astype(BF16)
            wu_ref[j] = stage_ref[1, slot].astype(BF16)
            wd_ref[j] = stage_d_ref[slot].astype(BF16)

            @pl.when(j + STAGE_SLOTS < N_FF_CHUNKS)
            def _():
                for c in copies(j + STAGE_SLOTS, slot):
                    c.start()

            acc_ref[...] += chunk(j)
            return carry

        lax.fori_loop(0, N_FF_CHUNKS, staged_chunk, 0)

    def unrolled_chunks():
        for j in range(N_FF_CHUNKS):
            if j == 0:
                acc_ref[...] = chunk(j)
            else:
                acc_ref[...] += chunk(j)

    def tile(run_chunks):
        x = jnp.where(is_ctx, x_refs[0][...], x_refs[1][...]) if split_in else x_refs[0][...]
        shift = m_ref[0, mod_base:mod_base + 1, :]
        scale = m_ref[0, mod_base + 1:mod_base + 2, :]
        gate = m_ref[0, mod_base + 2:mod_base + 3, :]
        z_ref[...] = _rms_modulate(x, g_ref[...], shift, scale).astype(BF16)
        run_chunks()
        y = x + (0.5 * gate) * acc_ref[...]
        if not split_out:
            o_refs[0][...] = y
        else:
            ms = jnp.mean(y * y, axis=-1, keepdims=True)
            y = y * lax.rsqrt(ms + RMS_EPS) * gf_ref[...]

            @pl.when(is_ctx)
            def _():
                o_refs[0][...] = y

            @pl.when(jnp.logical_not(is_ctx))
            def _():
                o_refs[1][...] = y

    @pl.when(first_tile)
    def _():
        tile(staged_chunks)

    @pl.when(jnp.logical_not(first_tile))
    def _():
        tile(unrolled_chunks)


def _ffn(xs, mod, gains, wg, wu, wd, g_final, layer, sub, *, split_in=False, split_out=False):
    tm = SPLIT_TM if (split_in or split_out) else TM
    kern = functools.partial(_ffn_kernel, mod_base=0 if sub == 0 else 6, split_in=split_in, split_out=split_out,
                             layer=layer, ffn_sub=0 if sub == 0 else 1, tm=tm)
    row = pl.BlockSpec((tm, D_MODEL), lambda i: (i, 0))
    split_rows = [pl.BlockSpec((tm, D_MODEL), lambda i: (_ctx_tile(i, tm), 0)),
                  pl.BlockSpec((tm, D_MODEL), lambda i: (_lat_tile(i, tm), 0))]
    hbm = pl.BlockSpec(memory_space=pl.ANY)
    if split_out:
        out_shape = (jax.ShapeDtypeStruct((CTX_TOKENS, D_MODEL), F32), jax.ShapeDtypeStruct((LAT_TOKENS, D_MODEL), F32))
        out_specs = tuple(split_rows)
    else:
        out_shape = jax.ShapeDtypeStruct((TOKENS, D_MODEL), F32)
        out_specs = row
    return pl.pallas_call(
        kern,
        out_shape=out_shape,
        grid=(TOKENS // tm,),
        in_specs=(split_rows if split_in else [row]) + [
            _mod_spec(layer, tm),
            _gain_spec(layer, sub),
            hbm, hbm, hbm,
            pl.BlockSpec((1, D_MODEL), lambda i: (0, 0)),
        ],
        out_specs=out_specs,
        scratch_shapes=[
            pltpu.VMEM((tm, D_MODEL), BF16),
            pltpu.VMEM((tm, D_MODEL), F32),
            pltpu.VMEM((N_FF_CHUNKS, D_MODEL, FF_CHUNK), BF16),
            pltpu.VMEM((N_FF_CHUNKS, D_MODEL, FF_CHUNK), BF16),
            pltpu.VMEM((N_FF_CHUNKS, FF_CHUNK, D_MODEL), BF16),
            pltpu.VMEM((2, STAGE_SLOTS, D_MODEL, FF_CHUNK), F32),
            pltpu.VMEM((STAGE_SLOTS, FF_CHUNK, D_MODEL), F32),
            pltpu.SemaphoreType.DMA((FFN_WEIGHTS, STAGE_SLOTS)),
        ],
        compiler_params=_params(1),
        name="ffn",
    )(*xs, mod, gains, wg, wu, wd, g_final)


def _sgu_tile(u, v, g_ref, w_ref, b_ref, o_ref):
    u = _gelu_tanh(u)
    v = _gelu_tanh(v)
    ms = jnp.mean(v * v, axis=-1, keepdims=True)
    vn = v * lax.rsqrt(ms + RMS_EPS) * g_ref[...]
    group = lax.broadcasted_iota(jnp.int32, (CHUNK, SGU_WIDTH), 1) // SGU_GROUP_DIM
    w = w_ref[...].astype(BF16)
    bias = b_ref[...]
    for n in range(TM // CHUNK):
        rows = slice(n * CHUNK, (n + 1) * CHUNK)
        x = vn[rows, :]
        stacked = jnp.concatenate(
            [jnp.where(group == g, x, 0.0) for g in range(SGU_GROUPS)], axis=0).astype(BF16)
        mixed = _dot(w, stacked) + bias
        o_ref[rows, :] = (u[rows, :] * mixed).astype(BF16)


def _fnet_ctx_tile(f, chan_ref, pos_ref, o_ref):
    y = _dot(f, chan_ref[...].astype(BF16))
    pos = pos_ref[...].astype(BF16)
    for b in range(REQ_PER_TILE):
        rows = slice(b * SEQ, (b + 1) * SEQ)
        yc = y[rows, :FNET_WIDTH].astype(BF16)
        ys = y[rows, FNET_WIDTH:].astype(BF16)
        out = _dot(pos[:, :SEQ], yc) - _dot(pos[:, SEQ:], ys)
        o_ref[rows, :] = (out * FNET_CTX_SCALE).astype(BF16)


def _inproj_kernel(h_ref, m_ref, g_ref, w_hbm, kt_in, vt_in, sg_ref, sw_ref, sb_ref, chan_ref, pos_ref,
                   pa_ref, ob_ref, pf_ref, occ_ref, kt_ref, vt_ref,
                   w_ref, stage_ref, sem_ref, z_ref, kf_ref, vf_ref, rest_ref, *, layer):
    del kt_in, vt_in

    @pl.when(pl.program_id(0) == 0)
    def _():
        _stage_weights(_row_chunks(w_hbm.at[layer, :, pl.ds(0, PROJ_COLS)], w_ref, D_MODEL), stage_ref, sem_ref)

    is_ctx = pl.program_id(0) < CTX_TILES
    q_cols, k_cols, v_cols = (slice(c * ATTN_WIDTH, (c + 1) * ATTN_WIDTH) for c in range(3))
    rest_cols = slice(3 * ATTN_WIDTH, PROJ_COLS)

    def project(cols):
        return _dot(z_ref[...], w_ref[:, cols])

    def tile(write_cache):
        shift = m_ref[0, 3:4, :]
        scale = m_ref[0, 4:5, :]
        z_ref[...] = _rms_modulate(h_ref[...], g_ref[...], shift, scale).astype(BF16)
        for cols, full_ref, cache_ref in ((k_cols, kf_ref, kt_ref), (v_cols, vf_ref, vt_ref)):
            y = project(cols)
            pa_ref[:, cols] = y.astype(BF16)
            if write_cache:
                full_ref[...] = y
                for b in range(REQ_PER_TILE):
                    cache_ref[b] = full_ref[b * SEQ:(b + 1) * SEQ, :].T
        pa_ref[:, q_cols] = (project(q_cols) * ATTN_SCALE).astype(BF16)
        rest_ref[...] = project(rest_cols)
        _sgu_tile(rest_ref[:, :SGU_WIDTH], rest_ref[:, SGU_WIDTH:2 * SGU_WIDTH], sg_ref, sw_ref, sb_ref, ob_ref)
        f = rest_ref[:, 2 * SGU_WIDTH:].astype(BF16)
        pf_ref[...] = f
        if write_cache:
            _fnet_ctx_tile(f, chan_ref, pos_ref, occ_ref)

    @pl.when(is_ctx)
    def _():
        tile(True)

    @pl.when(jnp.logical_not(is_ctx))
    def _():
        tile(False)


def _inproj(h, mod, gains, w_in, kt, vt, sgu_norm, w_cat, bias_tile, chan, pos_ctx, layer):
    cache_shape = jax.ShapeDtypeStruct((BATCH, DEPTH, ATTN_WIDTH, SEQ), F32)
    cache_spec = pl.BlockSpec((REQ_PER_TILE, None, ATTN_WIDTH, SEQ), lambda i: (_ctx_tile(i), layer, 0, 0))
    row = lambda width: pl.BlockSpec((TM, width), lambda i: (i, 0))
    return pl.pallas_call(
        functools.partial(_inproj_kernel, layer=layer),
        out_shape=(jax.ShapeDtypeStruct((TOKENS, 3 * ATTN_WIDTH), BF16),
                   jax.ShapeDtypeStruct((TOKENS, SGU_WIDTH), BF16),
                   jax.ShapeDtypeStruct((TOKENS, FNET_WIDTH), BF16),
                   jax.ShapeDtypeStruct((CTX_TOKENS, FNET_WIDTH), BF16), cache_shape, cache_shape),
        grid=(N_TILES,),
        in_specs=[
            row(D_MODEL),
            _mod_spec(layer),
            _gain_spec(layer, 1),
            pl.BlockSpec(memory_space=pl.ANY),
            pl.BlockSpec(memory_space=pl.ANY),
            pl.BlockSpec(memory_space=pl.ANY),
            pl.BlockSpec((None, 1, SGU_WIDTH), lambda i: (layer, 0, 0)),
            pl.BlockSpec((None, CHUNK, SGU_GROUPS * CHUNK), lambda i: (layer, 0, 0)),
            pl.BlockSpec((None, CHUNK, SGU_WIDTH), lambda i: (layer, 0, 0)),
            pl.BlockSpec((FNET_WIDTH, 2 * FNET_WIDTH), lambda i: (0, 0)),
            pl.BlockSpec((SEQ, 2 * SEQ), lambda i: (0, 0)),
        ],
        out_specs=(row(3 * ATTN_WIDTH), row(SGU_WIDTH), row(FNET_WIDTH),
                   pl.BlockSpec((TM, FNET_WIDTH), lambda i: (_ctx_tile(i), 0)), cache_spec, cache_spec),
        input_output_aliases={4: 4, 5: 5},
        scratch_shapes=[
            pltpu.VMEM((D_MODEL, PROJ_COLS), BF16),
            pltpu.VMEM((STAGE_SLOTS, STAGE_ROWS, PROJ_COLS), F32),
            pltpu.SemaphoreType.DMA((STAGE_SLOTS,)),
            pltpu.VMEM((TM, D_MODEL), BF16),
            pltpu.VMEM((TM, ATTN_WIDTH), F32),
            pltpu.VMEM((TM, ATTN_WIDTH), F32),
            pltpu.VMEM((TM, PROJ_COLS - 3 * ATTN_WIDTH), F32),
        ],
        compiler_params=_params(1),
        name="inproj",
    )(h, mod, gains, w_in, kt, vt, sgu_norm, w_cat, bias_tile, chan, pos_ctx)


HEAD_PAIRS = N_HEADS // 2
PAIR_WIDTH = 2 * HEAD_DIM
PAIR_COLS = [slice(j * PAIR_WIDTH, (j + 1) * PAIR_WIDTH) for j in range(HEAD_PAIRS)]


def _first_of_pair(rows):
    return lax.broadcasted_iota(jnp.int32, (rows, PAIR_WIDTH), 1) < HEAD_DIM


def _pair_queries(q):
    first = _first_of_pair(q.shape[0])
    zero = jnp.zeros_like(q)
    return jnp.concatenate([jnp.where(first, q, zero), jnp.where(first, zero, q)], axis=0)


def _pair_outputs(o):
    m = o.shape[0] // 2
    return jnp.where(_first_of_pair(m), o[:m], o[m:])


def _ctx_attn_kernel(q_ref, kt_ref, v_ref, o_ref):
    def one_request(b, carry):
        rows = pl.ds(pl.multiple_of(b * SEQ, SEQ), SEQ)
        q = q_ref[rows, :]
        kt = kt_ref[b].astype(BF16)
        v = v_ref[rows, :]
        s = jnp.concatenate([_dot(_pair_queries(q[:, c]), kt[c, :]) for c in PAIR_COLS], axis=0)
        e = jnp.exp(s - jnp.max(s, axis=-1, keepdims=True))
        inv = 1.0 / jnp.sum(e, axis=-1, keepdims=True)
        p = e.astype(BF16)
        outs = []
        for j, c in enumerate(PAIR_COLS):
            pair = slice(2 * j * SEQ, 2 * (j + 1) * SEQ)
            outs.append(_pair_outputs(_dot(p[pair, :], v[:, c]) * inv[pair, :]))
        o_ref[rows, :] = jnp.concatenate(outs, axis=-1).astype(BF16)
        return carry

    lax.fori_loop(0, REQ_PER_TILE, one_request, 0, unroll=True)


def _ctx_attention(qkv, kt, layer):
    return pl.pallas_call(
        _ctx_attn_kernel,
        out_shape=jax.ShapeDtypeStruct((CTX_TOKENS, ATTN_WIDTH), BF16),
        grid=(CTX_TILES,),
        in_specs=[
            pl.BlockSpec((TM, ATTN_WIDTH), lambda i: (i, 0)),
            pl.BlockSpec((REQ_PER_TILE, None, ATTN_WIDTH, SEQ), lambda i: (i, layer, 0, 0)),
            pl.BlockSpec((TM, ATTN_WIDTH), lambda i: (i, 2)),
        ],
        out_specs=pl.BlockSpec((TM, ATTN_WIDTH), lambda i: (i, 0)),
        compiler_params=_params(1),
        name="ctx_attention",
    )(qkv, kt, qkv)


N_DROW = 2 * WIN_H - 1
N_DCOL = 2 * WIN_W - 1
N_DROW_PAIRS = N_DROW - 1
KEY_ROWS = WIN_H
KEYS_LOCAL = KEY_ROWS * GRID_W
RPB_LANES = 2 * GRID_W


def _build_bias_table(rpb_ref, tab_ref):
    first = _first_of_pair(GRID_W)
    for h in range(N_HEADS):
        for d in range(N_DROW_PAIRS):
            lo = jnp.broadcast_to(rpb_ref[h, d:d + 1, :], (GRID_W, RPB_LANES))
            hi = jnp.broadcast_to(rpb_ref[h, d + 1:d + 2, :], (GRID_W, RPB_LANES))
            lo = pltpu.roll(lo, RPB_LANES - (WIN_W - 1), 1, stride=1, stride_axis=0)
            hi = pltpu.roll(hi, GRID_W - (WIN_W - 1), 1, stride=1, stride_axis=0)
            tab_ref[h, d] = jnp.where(first, lo, hi)


def _natten_kernel(qb_ref, kb_ref, vb_ref, kc_ref, vc_ref, rpb_ref, o_ref, tab_ref):
    @pl.when(pl.program_id(0) == 0)
    def _():
        _build_bias_table(rpb_ref, tab_ref)

    kc = kc_ref[...].astype(BF16)
    vc = vc_ref[...].astype(BF16)
    qcol = lax.broadcasted_iota(jnp.int32, (N_HEADS * GRID_W, KEYS_LOCAL), 0) % GRID_W
    kcol = lax.broadcasted_iota(jnp.int32, (N_HEADS * GRID_W, KEYS_LOCAL), 1) % GRID_W
    start = jnp.clip(qcol - WIN_W // 2, 0, GRID_W - WIN_W)
    col_ok = (kcol >= start) & (kcol < start + WIN_W)

    def one_row(r, carry):
        r0 = jnp.clip(r - KEY_ROWS // 2, 0, GRID_ROWS - KEY_ROWS)
        d0 = r0 - r + (WIN_H - 1)
        qrows = pl.ds(pl.multiple_of(r * GRID_W, GRID_W), GRID_W)
        krows = pl.ds(pl.multiple_of(r0 * GRID_W, GRID_W), KEYS_LOCAL)
        q = qb_ref[qrows, :]
        k = jnp.concatenate([kb_ref[krows, :], kc], axis=0)
        v = jnp.concatenate([vb_ref[krows, :], vc], axis=0)
        s = jnp.concatenate([_dot_nt(_pair_queries(q[:, c]), k[:, c]) for c in PAIR_COLS], axis=0)
        bias = jnp.concatenate(
            [jnp.concatenate([tab_ref[h, d0 + 2 * i] for i in range(KEY_ROWS // 2)], axis=-1)
             for h in range(N_HEADS)], axis=0)
        s_loc = jnp.where(col_ok, s[:, :KEYS_LOCAL] + bias, NEG_INF)
        s_ctx = s[:, KEYS_LOCAL:]
        mx = jnp.maximum(jnp.max(s_loc, axis=-1, keepdims=True), jnp.max(s_ctx, axis=-1, keepdims=True))
        e_loc = jnp.exp(s_loc - mx)
        e_ctx = jnp.exp(s_ctx - mx)
        inv = 1.0 / (jnp.sum(e_loc, axis=-1, keepdims=True) + jnp.sum(e_ctx, axis=-1, keepdims=True))
        p = jnp.concatenate([e_loc, e_ctx], axis=-1).astype(BF16)
        outs = []
        for j, c in enumerate(PAIR_COLS):
            pair = slice(2 * j * GRID_W, 2 * (j + 1) * GRID_W)
            outs.append(_pair_outputs(_dot(p[pair, :], v[:, c]) * inv[pair, :]))
        o_ref[qrows, :] = jnp.concatenate(outs, axis=-1).astype(BF16)
        return carry

    lax.fori_loop(0, GRID_ROWS, one_row, 0, unroll=4)


def _natten(qkv, cache_k, cache_v, rpb_pad, layer):
    lat0 = CTX_TOKENS // DEC_SEQ
    spec = lambda c: pl.BlockSpec((DEC_SEQ, ATTN_WIDTH), lambda b, c=c: (lat0 + b, c))
    cache_spec = pl.BlockSpec((None, None, PAST_LEN, ATTN_WIDTH), lambda b: (b, layer, 0, 0))
    return pl.pallas_call(
        _natten_kernel,
        out_shape=jax.ShapeDtypeStruct((LAT_TOKENS, ATTN_WIDTH), BF16),
        grid=(DEC_BATCH,),
        in_specs=[spec(0), spec(1), spec(2), cache_spec, cache_spec,
                  pl.BlockSpec((None, N_HEADS, N_DROW, RPB_LANES), lambda b: (layer, 0, 0, 0))],
        out_specs=pl.BlockSpec((DEC_SEQ, ATTN_WIDTH), lambda b: (b, 0)),
        scratch_shapes=[pltpu.VMEM((N_HEADS, N_DROW_PAIRS, GRID_W, RPB_LANES), F32)],
        compiler_params=_params(1),
        name="natten",
    )(qkv, qkv, qkv, cache_k, cache_v, rpb_pad)


def _dft_tables(n):
    k = np.arange(n, dtype=np.int64)
    ang = 2.0 * np.pi * ((k[:, None] * k[None, :]) % n).astype(np.float64) / n
    return np.cos(ang), np.sin(ang)


def _channel_tables():
    c, s = _dft_tables(FNET_GROUP_DIM)
    eye = np.eye(FNET_GROUPS)
    return np.concatenate([np.kron(eye, c), np.kron(eye, s)], axis=1)


_CHAN_TABLE = _channel_tables().astype(np.float32)
_CTX_POS_TABLE = np.concatenate(_dft_tables(SEQ), axis=1).astype(np.float32)
_LAT_POS_TABLE = np.concatenate(_dft_tables(DEC_SEQ), axis=1).astype(np.float32)
FNET_CTX_SCALE = np.float32(1.0 / np.sqrt(SEQ * FNET_GROUP_DIM))
FNET_LAT_SCALE = np.float32(1.0 / np.sqrt(DEC_SEQ * FNET_GROUP_DIM))


FNET_LAT_ROWS = 256


def _fnet_lat_kernel(f0_ref, f1_ref, chan_ref, pos_ref, o_ref, yc_ref, ys_ref):
    @pl.when(pl.program_id(0) == 0)
    def _():
        for b, f_ref in enumerate((f0_ref, f1_ref)):
            y = _dot(f_ref[...], chan_ref[...].astype(BF16))
            cols = slice(b * FNET_WIDTH, (b + 1) * FNET_WIDTH)
            yc_ref[:, cols] = y[:, :FNET_WIDTH].astype(BF16)
            ys_ref[:, cols] = y[:, FNET_WIDTH:].astype(BF16)

    out = (_dot(pos_ref[:, :DEC_SEQ].astype(BF16), yc_ref[...])
           - _dot(pos_ref[:, DEC_SEQ:].astype(BF16), ys_ref[...]))
    out = out * FNET_LAT_SCALE
    for b in range(DEC_BATCH):
        o_ref[b] = out[:, b * FNET_WIDTH:(b + 1) * FNET_WIDTH].astype(BF16)


def _fnet_lat(f, chan, pos):
    lat0 = CTX_TOKENS // DEC_SEQ
    return pl.pallas_call(
        _fnet_lat_kernel,
        out_shape=jax.ShapeDtypeStruct((DEC_BATCH, DEC_SEQ, FNET_WIDTH), BF16),
        grid=(DEC_SEQ // FNET_LAT_ROWS,),
        in_specs=[
            pl.BlockSpec((DEC_SEQ, FNET_WIDTH), lambda j: (lat0, 0)),
            pl.BlockSpec((DEC_SEQ, FNET_WIDTH), lambda j: (lat0 + 1, 0)),
            pl.BlockSpec((FNET_WIDTH, 2 * FNET_WIDTH), lambda j: (0, 0)),
            pl.BlockSpec((FNET_LAT_ROWS, 2 * DEC_SEQ), lambda j: (j, 0)),
        ],
        out_specs=pl.BlockSpec((DEC_BATCH, FNET_LAT_ROWS, FNET_WIDTH), lambda j: (0, j, 0)),
        scratch_shapes=[pltpu.VMEM((DEC_SEQ, DEC_BATCH * FNET_WIDTH), BF16)] * 2,
        compiler_params=_params(1),
        name="fnet_lat",
    )(f, f, chan, pos)


def _merge_kernel(h_ref, m_ref, g_ref, w_in_hbm, bg_ref, oac_ref, oal_ref, ob_ref, occ_ref, ocl_ref,
                  pa_hbm, pb_hbm, pc_hbm, wo_hbm, o_ref, wg_ref, pa_ref, pb_ref, pc_ref, wo_ref, stage_ref, sem_ref,
                  *, layer):
    @pl.when(pl.program_id(0) == 0)
    def _():
        jobs = (_row_chunks(w_in_hbm.at[layer, :, pl.ds(PROJ_COLS, GATE_COLS)], wg_ref, D_MODEL)
                + _row_chunks(pa_hbm.at[layer], pa_ref, ATTN_WIDTH)
                + _row_chunks(pb_hbm.at[layer], pb_ref, SGU_WIDTH)
                + _row_chunks(pc_hbm.at[layer], pc_ref, FNET_WIDTH)
                + _row_chunks(wo_hbm.at[layer], wo_ref, D_MODEL))
        _stage_weights(jobs, stage_ref, sem_ref)

    ctx = pl.program_id(0) < CTX_TILES
    x = h_ref[...]
    shift = m_ref[0, 3:4, :]
    scale = m_ref[0, 4:5, :]
    gate = m_ref[0, 5:6, :]
    z = _rms_modulate(x, g_ref[...], shift, scale).astype(BF16)
    oa = jnp.where(ctx, oac_ref[...], oal_ref[...])
    oc = jnp.where(ctx, occ_ref[...], ocl_ref[...])
    branches = ((oa, pa_ref), (ob_ref[...], pb_ref), (oc, pc_ref))
    mix = None
    for j, (o, p_ref) in enumerate(branches):
        cols = slice(j * D_MODEL, (j + 1) * D_MODEL)
        gj = _sigmoid(_dot(z, wg_ref[:, cols]) + bg_ref[j:j + 1, :])
        term = gj * _dot(o, p_ref[...])
        mix = term if mix is None else mix + term
    o_ref[...] = x + gate * _dot(mix.astype(BF16), wo_ref[...])


def _merge(h, mod, gains, w_in, b_gate, oa_ctx, oa_lat, ob, oc_ctx, oc_lat, p_attn, p_sgu, p_fnet, w_out, layer):
    lw = lambda i: (layer, 0, 0)
    ctx_tile = lambda i: (_ctx_tile(i), 0)
    lat_tile = lambda i: (_lat_tile(i), 0)
    hbm = pl.BlockSpec(memory_space=pl.ANY)
    return pl.pallas_call(
        functools.partial(_merge_kernel, layer=layer),
        out_shape=jax.ShapeDtypeStruct((TOKENS, D_MODEL), F32),
        grid=(N_TILES,),
        in_specs=[
            pl.BlockSpec((TM, D_MODEL), lambda i: (i, 0)),
            _mod_spec(layer),
            _gain_spec(layer, 1),
            hbm,
            pl.BlockSpec((None, N_BRANCH, D_MODEL), lw),
            pl.BlockSpec((TM, ATTN_WIDTH), ctx_tile),
            pl.BlockSpec((TM, ATTN_WIDTH), lat_tile),
            pl.BlockSpec((TM, SGU_WIDTH), lambda i: (i, 0)),
            pl.BlockSpec((TM, FNET_WIDTH), ctx_tile),
            pl.BlockSpec((TM, FNET_WIDTH), lat_tile),
            hbm, hbm, hbm, hbm,
        ],
        out_specs=pl.BlockSpec((TM, D_MODEL), lambda i: (i, 0)),
        scratch_shapes=[
            pltpu.VMEM((D_MODEL, GATE_COLS), BF16),
            pltpu.VMEM((ATTN_WIDTH, D_MODEL), BF16),
            pltpu.VMEM((SGU_WIDTH, D_MODEL), BF16),
            pltpu.VMEM((FNET_WIDTH, D_MODEL), BF16),
            pltpu.VMEM((D_MODEL, D_MODEL), BF16),
            pltpu.VMEM((STAGE_SLOTS, STAGE_ROWS, GATE_COLS), F32),
            pltpu.SemaphoreType.DMA((STAGE_SLOTS,)),
        ],
        compiler_params=_params(1),
        name="merge",
    )(h, mod, gains, w_in, b_gate, oa_ctx, oa_lat, ob, oc_ctx, oc_lat, p_attn, p_sgu, p_fnet, w_out)


def kernel(x_prompt, x_sample, cache_k, cache_v, c, c_ctx, w_mod, b_mod, g_norm, ffn_w_gate, ffn_w_up,
           ffn_w_down, w_in, b_gate, rpb, sgu_norm, sgu_w, sgu_b, p_attn, p_sgu, p_fnet, w_out, g_final):
    cond = jnp.concatenate([c_ctx[None, :], c, jnp.zeros((COND_ROWS - N_COND, D_MODEL), F32)], axis=0)
    mod = _modulation(cond, w_mod, b_mod).reshape(DEPTH, COND_ROWS, N_MOD, D_MODEL)
    rpb_pad = jnp.pad(rpb, ((0, 0), (0, 0), (0, 0), (0, RPB_LANES - N_DCOL)))
    cache_k = cache_k.reshape(DEC_BATCH, DEPTH, PAST_LEN, ATTN_WIDTH)
    cache_v = cache_v.reshape(DEC_BATCH, DEPTH, PAST_LEN, ATTN_WIDTH)
    gains = g_norm.reshape(DEPTH, 3, 1, D_MODEL)
    g_final2 = g_final.reshape(1, D_MODEL)

    sgu_gain = sgu_norm.reshape(DEPTH, 1, SGU_WIDTH)
    w_cat = jnp.transpose(sgu_w, (0, 2, 1, 3)).reshape(DEPTH, CHUNK, SGU_GROUPS * CHUNK)
    bias_tile = jnp.repeat(jnp.transpose(sgu_b, (0, 2, 1)), SGU_GROUP_DIM, axis=2)

    chan = jnp.asarray(_CHAN_TABLE)
    pos_ctx = jnp.asarray(_CTX_POS_TABLE)
    pos_lat = jnp.asarray(_LAT_POS_TABLE)

    kt = jnp.zeros((BATCH, DEPTH, ATTN_WIDTH, SEQ), F32)
    vt = jnp.zeros((BATCH, DEPTH, ATTN_WIDTH, SEQ), F32)

    xs = (x_prompt.reshape(CTX_TOKENS, D_MODEL), x_sample.reshape(LAT_TOKENS, D_MODEL))
    for l in range(DEPTH):
        h = _ffn(xs, mod, gains, ffn_w_gate, ffn_w_up, ffn_w_down, g_final2, l, 0, split_in=(l == 0))
        qkv, ob, f, oc_ctx, kt, vt = _inproj(h, mod, gains, w_in, kt, vt, sgu_gain, w_cat, bias_tile, chan, pos_ctx, l)
        oa_ctx = _ctx_attention(qkv, kt, l)
        oa_lat = _natten(qkv, cache_k, cache_v, rpb_pad, l)
        oc_lat = _fnet_lat(f, chan, pos_lat).reshape(LAT_TOKENS, FNET_WIDTH)
        h = _merge(h, mod, gains, w_in, b_gate, oa_ctx, oa_lat, ob, oc_ctx, oc_lat, p_attn, p_sgu, p_fnet, w_out, l)
        xs = _ffn((h,), mod, gains, ffn_w_gate, ffn_w_up, ffn_w_down, g_final2, l, 2, split_out=(l == DEPTH - 1))
        xs = xs if l == DEPTH - 1 else (xs,)

    y_prompt, y_sample = xs
    to_cache = lambda t: jnp.transpose(t.reshape(BATCH, DEPTH, N_HEADS, HEAD_DIM, SEQ), (0, 1, 4, 2, 3))
    return (y_prompt.reshape(BATCH, SEQ, D_MODEL), y_sample.reshape(DEC_BATCH, DEC_SEQ, D_MODEL),
            to_cache(kt), to_cache(vt))
```

```python
import functools

import numpy as np
import jax
import jax.numpy as jnp
from jax import lax
from jax.experimental import pallas as pl
from jax.experimental.pallas import tpu as pltpu

D_MODEL = 1024
BATCH = 32
SEQ = 256
DEPTH = 4
DEC_BATCH = 2
DEC_SEQ = 2048
PAST_LEN = 256
GRID_W = 64
GRID_ROWS = DEC_SEQ // GRID_W
WIN_H = 8
WIN_W = 16
HEAD_DIM = 64
ATTN_WIDTH = 512
N_HEADS = 8
SGU_WIDTH = 256
SGU_GROUPS = 4
SGU_GROUP_DIM = 64
CHUNK = 128
FNET_WIDTH = 256
FNET_GROUPS = 4
FNET_GROUP_DIM = 64
N_BRANCH = 3
D_FF = 2816
N_MOD = 9
RMS_EPS = 1e-6
NEG_INF = -1e30

CTX_TOKENS = BATCH * SEQ
LAT_TOKENS = DEC_BATCH * DEC_SEQ
TOKENS = CTX_TOKENS + LAT_TOKENS
N_COND = 1 + DEC_BATCH
COND_ROWS = 8
PROJ_COLS = 3 * ATTN_WIDTH + 2 * SGU_WIDTH + FNET_WIDTH
GATE_COLS = N_BRANCH * D_MODEL

TM = 1024
SPLIT_TM = 512
N_TILES = TOKENS // TM
CTX_TILES = CTX_TOKENS // TM
REQ_PER_TILE = TM // SEQ
FF_CHUNK = 256
VMEM_LIMIT = 56 * 1024 * 1024

BF16 = jnp.bfloat16
F32 = jnp.float32
ATTN_SCALE = np.float32(HEAD_DIM ** -0.5)


def _cond_row(i, tm=TM):
    ctx_tiles = CTX_TOKENS // tm
    return jnp.where(i < ctx_tiles, 0, 1 + (i - ctx_tiles) // (DEC_SEQ // tm))


def _ctx_tile(i, tm=TM):
    return jnp.minimum(i, CTX_TOKENS // tm - 1)


def _lat_tile(i, tm=TM):
    return jnp.maximum(i - CTX_TOKENS // tm, 0)


def _params(n_axes):
    return pltpu.CompilerParams(dimension_semantics=("arbitrary",) * n_axes,
                                vmem_limit_bytes=VMEM_LIMIT)


def _dot(a, b):
    return jnp.dot(a, b, preferred_element_type=F32)


def _dot_nt(a, b):
    return lax.dot_general(a, b, (((1,), (1,)), ((), ())), preferred_element_type=F32)


def _rms_modulate(x, g, shift, scale):
    ms = jnp.mean(x * x, axis=-1, keepdims=True)
    y = x * lax.rsqrt(ms + RMS_EPS) * g
    return y * (1.0 + scale) + shift


def _silu(x):
    return x * (1.0 / (1.0 + jnp.exp(-x)))


def _sigmoid(x):
    return 1.0 / (1.0 + jnp.exp(-x))


def _gelu_tanh(x):
    c = np.float32(np.sqrt(2.0 / np.pi))
    return x * (0.5 * (1.0 + jnp.tanh(c * (x + np.float32(0.044715) * (x * x * x)))))


MOD_COLS = N_MOD * D_MODEL
MOD_BLOCK = 4608


def _mod_kernel(c_ref, w_ref, b_ref, o_ref):
    a = _silu(c_ref[...]).astype(BF16)
    o_ref[0] = _dot(a, w_ref[0].astype(BF16)) + b_ref[0]


def _modulation(cond, w_mod, b_mod):
    return pl.pallas_call(
        _mod_kernel,
        out_shape=jax.ShapeDtypeStruct((DEPTH, COND_ROWS, MOD_COLS), F32),
        grid=(DEPTH, MOD_COLS // MOD_BLOCK),
        in_specs=[
            pl.BlockSpec((COND_ROWS, D_MODEL), lambda l, j: (0, 0)),
            pl.BlockSpec((1, D_MODEL, MOD_BLOCK), lambda l, j: (l, 0, j)),
            pl.BlockSpec((1, 1, MOD_BLOCK), lambda l, j: (l, 0, j)),
        ],
        out_specs=pl.BlockSpec((1, COND_ROWS, MOD_BLOCK), lambda l, j: (l, 0, j)),
        compiler_params=_params(2),
        name="modulation",
    )(cond, w_mod, b_mod.reshape(DEPTH, 1, MOD_COLS))


def _mod_spec(layer, tm=TM):
    return pl.BlockSpec((None, 1, N_MOD, D_MODEL), lambda i: (layer, _cond_row(i, tm), 0, 0))


def _gain_spec(layer, sub):
    return pl.BlockSpec((None, None, 1, D_MODEL), lambda i: (layer, sub, 0, 0))


STAGE_ROWS = 256
STAGE_SLOTS = 2


def _stage_weights(jobs, stage_ref, sem_ref):
    def copy(k):
        src = jobs[k][0]
        rows, cols = src.shape
        return pltpu.make_async_copy(src, stage_ref.at[k % STAGE_SLOTS, :rows, :cols], sem_ref.at[k % STAGE_SLOTS])

    copy(0).start()
    for k, (src, dst) in enumerate(jobs):
        if k + 1 < len(jobs):
            copy(k + 1).start()
        copy(k).wait()
        rows, cols = src.shape
        dst[...] = stage_ref[k % STAGE_SLOTS, :rows, :cols].astype(BF16)


def _row_chunks(src, dst, rows):
    return [(src.at[pl.ds(r, STAGE_ROWS), :], dst.at[pl.ds(r, STAGE_ROWS), :]) for r in range(0, rows, STAGE_ROWS)]


N_FF_CHUNKS = D_FF // FF_CHUNK
FFN_WEIGHTS = 3


def _ffn_kernel(*refs, mod_base, split_in, split_out, layer, ffn_sub, tm):
    n_in = 2 if split_in else 1
    x_refs, (m_ref, g_ref, wg_hbm, wu_hbm, wd_hbm, gf_ref) = refs[:n_in], refs[n_in:n_in + 6]
    n_out = 2 if split_out else 1
    o_refs = refs[n_in + 6:n_in + 6 + n_out]
    z_ref, acc_ref, wg_ref, wu_ref, wd_ref, stage_ref, stage_d_ref, sem_ref = refs[n_in + 6 + n_out:]
    first_tile = pl.program_id(0) == 0
    is_ctx = pl.program_id(0) < CTX_TOKENS // tm

    def chunk(j):
        z = z_ref[...]
        act = (_silu(_dot(z, wg_ref[j])) * _dot(z, wu_ref[j])).astype(BF16)
        return _dot(act, wd_ref[j])

    def copies(j, slot):
        cols = pl.ds(pl.multiple_of(j * FF_CHUNK, FF_CHUNK), FF_CHUNK)
        srcs = (wg_hbm.at[layer, ffn_sub, :, cols], wu_hbm.at[layer, ffn_sub, :, cols], wd_hbm.at[layer, ffn_sub, cols, :])
        dsts = (stage_ref.at[0, slot], stage_ref.at[1, slot], stage_d_ref.at[slot])
        return [pltpu.make_async_copy(src, dst, sem_ref.at[w, slot]) for w, (src, dst) in enumerate(zip(srcs, dsts))]

    def staged_chunks():
        for j in range(STAGE_SLOTS):
            for c in copies(j, j):
                c.start()
        acc_ref[...] = jnp.zeros_like(acc_ref)

        def staged_chunk(j, carry):
            slot = j % STAGE_SLOTS
            for c in copies(j, slot):
                c.wait()
            wg_ref[j] = stage_ref[0, slot].astype(BF16)
            wu_ref[j] = stage_ref[1, slot].astype(BF16)
            wd_ref[j] = stage_d_ref[slot].astype(BF16)

            @pl.when(j + STAGE_SLOTS < N_FF_CHUNKS)
            def _():
                for c in copies(j + STAGE_SLOTS, slot):
                    c.start()

            acc_ref[...] += chunk(j)
            return carry

        lax.fori_loop(0, N_FF_CHUNKS, staged_chunk, 0)

    def unrolled_chunks():
        for j in range(N_FF_CHUNKS):
            if j == 0:
                acc_ref[...] = chunk(j)
            else:
                acc_ref[...] += chunk(j)

    def tile(run_chunks):
        x = jnp.where(is_ctx, x_refs[0][...], x_refs[1][...]) if split_in else x_refs[0][...]
        shift = m_ref[0, mod_base:mod_base + 1, :]
        scale = m_ref[0, mod_base + 1:mod_base + 2, :]
        gate = m_ref[0, mod_base + 2:mod_base + 3, :]
        z_ref[...] = _rms_modulate(x, g_ref[...], shift, scale).astype(BF16)
        run_chunks()
        y = x + (0.5 * gate) * acc_ref[...]
        if not split_out:
            o_refs[0][...] = y
        else:
            ms = jnp.mean(y * y, axis=-1, keepdims=True)
            y = y * lax.rsqrt(ms + RMS_EPS) * gf_ref[...]

            @pl.when(is_ctx)
            def _():
                o_refs[0][...] = y

            @pl.when(jnp.logical_not(is_ctx))
            def _():
                o_refs[1][...] = y

    @pl.when(first_tile)
    def _():
        tile(staged_chunks)

    @pl.when(jnp.logical_not(first_tile))
    def _():
        tile(unrolled_chunks)


def _ffn(xs, mod, gains, wg, wu, wd, g_final, layer, sub, *, split_in=False, split_out=False):
    tm = SPLIT_TM if (split_in or split_out) else TM
    kern = functools.partial(_ffn_kernel, mod_base=0 if sub == 0 else 6, split_in=split_in, split_out=split_out,
                             layer=layer, ffn_sub=0 if sub == 0 else 1, tm=tm)
    row = pl.BlockSpec((tm, D_MODEL), lambda i: (i, 0))
    split_rows = [pl.BlockSpec((tm, D_MODEL), lambda i: (_ctx_tile(i, tm), 0)),
                  pl.BlockSpec((tm, D_MODEL), lambda i: (_lat_tile(i, tm), 0))]
    hbm = pl.BlockSpec(memory_space=pl.ANY)
    if split_out:
        out_shape = (jax.ShapeDtypeStruct((CTX_TOKENS, D_MODEL), F32), jax.ShapeDtypeStruct((LAT_TOKENS, D_MODEL), F32))
        out_specs = tuple(split_rows)
    else:
        out_shape = jax.ShapeDtypeStruct((TOKENS, D_MODEL), F32)
        out_specs = row
    return pl.pallas_call(
        kern,
        out_shape=out_shape,
        grid=(TOKENS // tm,),
        in_specs=(split_rows if split_in else [row]) + [
            _mod_spec(layer, tm),
            _gain_spec(layer, sub),
            hbm, hbm, hbm,
            pl.BlockSpec((1, D_MODEL), lambda i: (0, 0)),
        ],
        out_specs=out_specs,
        scratch_shapes=[
            pltpu.VMEM((tm, D_MODEL), BF16),
            pltpu.VMEM((tm, D_MODEL), F32),
            pltpu.VMEM((N_FF_CHUNKS, D_MODEL, FF_CHUNK), BF16),
            pltpu.VMEM((N_FF_CHUNKS, D_MODEL, FF_CHUNK), BF16),
            pltpu.VMEM((N_FF_CHUNKS, FF_CHUNK, D_MODEL), BF16),
            pltpu.VMEM((2, STAGE_SLOTS, D_MODEL, FF_CHUNK), F32),
            pltpu.VMEM((STAGE_SLOTS, FF_CHUNK, D_MODEL), F32),
            pltpu.SemaphoreType.DMA((FFN_WEIGHTS, STAGE_SLOTS)),
        ],
        compiler_params=_params(1),
        name="ffn",
    )(*xs, mod, gains, wg, wu, wd, g_final)


def _sgu_tile(u, v, g_ref, w_ref, b_ref, o_ref):
    u = _gelu_tanh(u)
    v = _gelu_tanh(v)
    ms = jnp.mean(v * v, axis=-1, keepdims=True)
    vn = v * lax.rsqrt(ms + RMS_EPS) * g_ref[...]
    group = lax.broadcasted_iota(jnp.int32, (CHUNK, SGU_WIDTH), 1) // SGU_GROUP_DIM
    w = w_ref[...].astype(BF16)
    bias = b_ref[...]
    for n in range(TM // CHUNK):
        rows = slice(n * CHUNK, (n + 1) * CHUNK)
        x = vn[rows, :]
        stacked = jnp.concatenate(
            [jnp.where(group == g, x, 0.0) for g in range(SGU_GROUPS)], axis=0).astype(BF16)
        mixed = _dot(w, stacked) + bias
        o_ref[rows, :] = (u[rows, :] * mixed).astype(BF16)


def _fnet_ctx_tile(f, chan_ref, pos_ref, o_ref):
    y = _dot(f, chan_ref[...].astype(BF16))
    pos = pos_ref[...].astype(BF16)
    for b in range(REQ_PER_TILE):
        rows = slice(b * SEQ, (b + 1) * SEQ)
        yc = y[rows, :FNET_WIDTH].astype(BF16)
        ys = y[rows, FNET_WIDTH:].astype(BF16)
        out = _dot(pos[:, :SEQ], yc) - _dot(pos[:, SEQ:], ys)
        o_ref[rows, :] = (out * FNET_CTX_SCALE).astype(BF16)


def _inproj_kernel(h_ref, m_ref, g_ref, w_hbm, kt_in, vt_in, sg_ref, sw_ref, sb_ref, chan_ref, pos_ref,
                   pa_ref, ob_ref, pf_ref, occ_ref, oac_ref, kt_ref, vt_ref,
                   w_ref, stage_ref, sem_ref, z_ref, kf_ref, vf_ref, *, layer):
    del kt_in, vt_in

    @pl.when(pl.program_id(0) == 0)
    def _():
        _stage_weights(_row_chunks(w_hbm.at[layer, :, pl.ds(0, PROJ_COLS)], w_ref, D_MODEL), stage_ref, sem_ref)

    is_ctx = pl.program_id(0) < CTX_TILES
    q_cols, k_cols, v_cols = (slice(c * ATTN_WIDTH, (c + 1) * ATTN_WIDTH) for c in range(3))
    rest_cols = slice(3 * ATTN_WIDTH, PROJ_COLS)

    def project(cols):
        return _dot(z_ref[...], w_ref[:, cols])

    def tile(write_cache):
        shift = m_ref[0, 3:4, :]
        scale = m_ref[0, 4:5, :]
        z_ref[...] = _rms_modulate(h_ref[...], g_ref[...], shift, scale).astype(BF16)
        for cols, full_ref, cache_ref in ((k_cols, kf_ref, kt_ref), (v_cols, vf_ref, vt_ref)):
            y = project(cols)
            pa_ref[:, cols] = y.astype(BF16)
            if write_cache:
                full_ref[...] = y
                for b in range(REQ_PER_TILE):
                    cache_ref[b] = full_ref[b * SEQ:(b + 1) * SEQ, :].T
        pa_ref[:, q_cols] = (project(q_cols) * ATTN_SCALE).astype(BF16)
        if write_cache:
            _ctx_attention_tile(pa_ref, kt_ref, oac_ref)
        rest = project(rest_cols)
        _sgu_tile(rest[:, :SGU_WIDTH], rest[:, SGU_WIDTH:2 * SGU_WIDTH], sg_ref, sw_ref, sb_ref, ob_ref)
        f = rest[:, 2 * SGU_WIDTH:].astype(BF16)
        pf_ref[...] = f
        if write_cache:
            _fnet_ctx_tile(f, chan_ref, pos_ref, occ_ref)

    @pl.when(is_ctx)
    def _():
        tile(True)

    @pl.when(jnp.logical_not(is_ctx))
    def _():
        tile(False)


def _inproj(h, mod, gains, w_in, kt, vt, sgu_norm, w_cat, bias_tile, chan, pos_ctx, layer):
    cache_shape = jax.ShapeDtypeStruct((BATCH, DEPTH, ATTN_WIDTH, SEQ), F32)
    cache_spec = pl.BlockSpec((REQ_PER_TILE, None, ATTN_WIDTH, SEQ), lambda i: (_ctx_tile(i), layer, 0, 0))
    row = lambda width: pl.BlockSpec((TM, width), lambda i: (i, 0))
    return pl.pallas_call(
        functools.partial(_inproj_kernel, layer=layer),
        out_shape=(jax.ShapeDtypeStruct((TOKENS, 3 * ATTN_WIDTH), BF16),
                   jax.ShapeDtypeStruct((TOKENS, SGU_WIDTH), BF16),
                   jax.ShapeDtypeStruct((TOKENS, FNET_WIDTH), BF16),
                   jax.ShapeDtypeStruct((CTX_TOKENS, FNET_WIDTH), BF16),
                   jax.ShapeDtypeStruct((CTX_TOKENS, ATTN_WIDTH), BF16), cache_shape, cache_shape),
        grid=(N_TILES,),
        in_specs=[
            row(D_MODEL),
            _mod_spec(layer),
            _gain_spec(layer, 1),
            pl.BlockSpec(memory_space=pl.ANY),
            pl.BlockSpec(memory_space=pl.ANY),
            pl.BlockSpec(memory_space=pl.ANY),
            pl.BlockSpec((None, 1, SGU_WIDTH), lambda i: (layer, 0, 0)),
            pl.BlockSpec((None, CHUNK, SGU_GROUPS * CHUNK), lambda i: (layer, 0, 0)),
            pl.BlockSpec((None, CHUNK, SGU_WIDTH), lambda i: (layer, 0, 0)),
            pl.BlockSpec((FNET_WIDTH, 2 * FNET_WIDTH), lambda i: (0, 0)),
            pl.BlockSpec((SEQ, 2 * SEQ), lambda i: (0, 0)),
        ],
        out_specs=(row(3 * ATTN_WIDTH), row(SGU_WIDTH), row(FNET_WIDTH),
                   pl.BlockSpec((TM, FNET_WIDTH), lambda i: (_ctx_tile(i), 0)),
                   pl.BlockSpec((TM, ATTN_WIDTH), lambda i: (_ctx_tile(i), 0)), cache_spec, cache_spec),
        input_output_aliases={4: 5, 5: 6},
        scratch_shapes=[
            pltpu.VMEM((D_MODEL, PROJ_COLS), BF16),
            pltpu.VMEM((STAGE_SLOTS, STAGE_ROWS, PROJ_COLS), F32),
            pltpu.SemaphoreType.DMA((STAGE_SLOTS,)),
            pltpu.VMEM((TM, D_MODEL), BF16),
            pltpu.VMEM((TM, ATTN_WIDTH), F32),
            pltpu.VMEM((TM, ATTN_WIDTH), F32),
        ],
        compiler_params=_params(1),
        name="inproj",
    )(h, mod, gains, w_in, kt, vt, sgu_norm, w_cat, bias_tile, chan, pos_ctx)


HEAD_PAIRS = N_HEADS // 2
PAIR_WIDTH = 2 * HEAD_DIM
PAIR_COLS = [slice(j * PAIR_WIDTH, (j + 1) * PAIR_WIDTH) for j in range(HEAD_PAIRS)]


def _first_of_pair(rows):
    return lax.broadcasted_iota(jnp.int32, (rows, PAIR_WIDTH), 1) < HEAD_DIM


def _pair_queries(q):
    first = _first_of_pair(q.shape[0])
    zero = jnp.zeros_like(q)
    return jnp.concatenate([jnp.where(first, q, zero), jnp.where(first, zero, q)], axis=0)


def _pair_outputs(o):
    m = o.shape[0] // 2
    return jnp.where(_first_of_pair(m), o[:m], o[m:])


def _ctx_attention_tile(qkv_ref, kt_ref, o_ref):
    for b in range(REQ_PER_TILE):
        rows = slice(b * SEQ, (b + 1) * SEQ)
        q = qkv_ref[rows, :ATTN_WIDTH]
        kt = kt_ref[b].astype(BF16)
        v = qkv_ref[rows, 2 * ATTN_WIDTH:]
        s = jnp.concatenate([_dot(_pair_queries(q[:, c]), kt[c, :]) for c in PAIR_COLS], axis=0)
        e = jnp.exp(s - jnp.max(s, axis=-1, keepdims=True))
        inv = 1.0 / jnp.sum(e, axis=-1, keepdims=True)
        p = e.astype(BF16)
        outs = []
        for j, c in enumerate(PAIR_COLS):
            pair = slice(2 * j * SEQ, 2 * (j + 1) * SEQ)
            outs.append(_pair_outputs(_dot(p[pair, :], v[:, c]) * inv[pair, :]))
        o_ref[rows, :] = jnp.concatenate(outs, axis=-1).astype(BF16)


N_DROW = 2 * WIN_H - 1
N_DCOL = 2 * WIN_W - 1
N_DROW_PAIRS = N_DROW - 1
KEY_ROWS = WIN_H
KEYS_LOCAL = KEY_ROWS * GRID_W
RPB_LANES = 2 * GRID_W


def _build_bias_table(rpb_ref, tab_ref):
    first = _first_of_pair(GRID_W)
    for h in range(N_HEADS):
        for d in range(N_DROW_PAIRS):
            lo = jnp.broadcast_to(rpb_ref[h, d:d + 1, :], (GRID_W, RPB_LANES))
            hi = jnp.broadcast_to(rpb_ref[h, d + 1:d + 2, :], (GRID_W, RPB_LANES))
            lo = pltpu.roll(lo, RPB_LANES - (WIN_W - 1), 1, stride=1, stride_axis=0)
            hi = pltpu.roll(hi, GRID_W - (WIN_W - 1), 1, stride=1, stride_axis=0)
            tab_ref[h, d] = jnp.where(first, lo, hi)


def _natten_kernel(qb_ref, kb_ref, vb_ref, kc_ref, vc_ref, rpb_ref, o_ref, tab_ref):
    @pl.when(pl.program_id(0) == 0)
    def _():
        _build_bias_table(rpb_ref, tab_ref)

    kc = kc_ref[...].astype(BF16)
    vc = vc_ref[...].astype(BF16)
    qcol = lax.broadcasted_iota(jnp.int32, (N_HEADS * GRID_W, KEYS_LOCAL), 0) % GRID_W
    kcol = lax.broadcasted_iota(jnp.int32, (N_HEADS * GRID_W, KEYS_LOCAL), 1) % GRID_W
    start = jnp.clip(qcol - WIN_W // 2, 0, GRID_W - WIN_W)
    col_ok = (kcol >= start) & (kcol < start + WIN_W)

    def one_row(r, carry):
        r0 = jnp.clip(r - KEY_ROWS // 2, 0, GRID_ROWS - KEY_ROWS)
        d0 = r0 - r + (WIN_H - 1)
        qrows = pl.ds(pl.multiple_of(r * GRID_W, GRID_W), GRID_W)
        krows = pl.ds(pl.multiple_of(r0 * GRID_W, GRID_W), KEYS_LOCAL)
        q = qb_ref[qrows, :]
        k = jnp.concatenate([kb_ref[krows, :], kc], axis=0)
        v = jnp.concatenate([vb_ref[krows, :], vc], axis=0)
        s = jnp.concatenate([_dot_nt(_pair_queries(q[:, c]), k[:, c]) for c in PAIR_COLS], axis=0)
        bias = jnp.concatenate(
            [jnp.concatenate([tab_ref[h, d0 + 2 * i] for i in range(KEY_ROWS // 2)], axis=-1)
             for h in range(N_HEADS)], axis=0)
        s_loc = jnp.where(col_ok, s[:, :KEYS_LOCAL] + bias, NEG_INF)
        s_ctx = s[:, KEYS_LOCAL:]
        mx = jnp.maximum(jnp.max(s_loc, axis=-1, keepdims=True), jnp.max(s_ctx, axis=-1, keepdims=True))
        e_loc = jnp.exp(s_loc - mx)
        e_ctx = jnp.exp(s_ctx - mx)
        inv = 1.0 / (jnp.sum(e_loc, axis=-1, keepdims=True) + jnp.sum(e_ctx, axis=-1, keepdims=True))
        p = jnp.concatenate([e_loc, e_ctx], axis=-1).astype(BF16)
        outs = []
        for j, c in enumerate(PAIR_COLS):
            pair = slice(2 * j * GRID_W, 2 * (j + 1) * GRID_W)
            outs.append(_pair_outputs(_dot(p[pair, :], v[:, c]) * inv[pair, :]))
        o_ref[qrows, :] = jnp.concatenate(outs, axis=-1).astype(BF16)
        return carry

    lax.fori_loop(0, GRID_ROWS, one_row, 0, unroll=4)


def _natten(qkv, cache_k, cache_v, rpb_pad, layer):
    lat0 = CTX_TOKENS // DEC_SEQ
    spec = lambda c: pl.BlockSpec((DEC_SEQ, ATTN_WIDTH), lambda b, c=c: (lat0 + b, c))
    cache_spec = pl.BlockSpec((None, None, PAST_LEN, ATTN_WIDTH), lambda b: (b, layer, 0, 0))
    return pl.pallas_call(
        _natten_kernel,
        out_shape=jax.ShapeDtypeStruct((LAT_TOKENS, ATTN_WIDTH), BF16),
        grid=(DEC_BATCH,),
        in_specs=[spec(0), spec(1), spec(2), cache_spec, cache_spec,
                  pl.BlockSpec((None, N_HEADS, N_DROW, RPB_LANES), lambda b: (layer, 0, 0, 0))],
        out_specs=pl.BlockSpec((DEC_SEQ, ATTN_WIDTH), lambda b: (b, 0)),
        scratch_shapes=[pltpu.VMEM((N_HEADS, N_DROW_PAIRS, GRID_W, RPB_LANES), F32)],
        compiler_params=_params(1),
        name="natten",
    )(qkv, qkv, qkv, cache_k, cache_v, rpb_pad)


def _dft_tables(n):
    k = np.arange(n, dtype=np.int64)
    ang = 2.0 * np.pi * ((k[:, None] * k[None, :]) % n).astype(np.float64) / n
    return np.cos(ang), np.sin(ang)


def _channel_tables():
    c, s = _dft_tables(FNET_GROUP_DIM)
    eye = np.eye(FNET_GROUPS)
    return np.concatenate([np.kron(eye, c), np.kron(eye, s)], axis=1)


_CHAN_TABLE = _channel_tables().astype(np.float32)
_CTX_POS_TABLE = np.concatenate(_dft_tables(SEQ), axis=1).astype(np.float32)
_LAT_POS_TABLE = np.concatenate(_dft_tables(DEC_SEQ), axis=1).astype(np.float32)
FNET_CTX_SCALE = np.float32(1.0 / np.sqrt(SEQ * FNET_GROUP_DIM))
FNET_LAT_SCALE = np.float32(1.0 / np.sqrt(DEC_SEQ * FNET_GROUP_DIM))


FNET_LAT_ROWS = 256


def _fnet_lat_kernel(f0_ref, f1_ref, chan_ref, pos_ref, o_ref, yc_ref, ys_ref):
    @pl.when(pl.program_id(0) == 0)
    def _():
        for b, f_ref in enumerate((f0_ref, f1_ref)):
            y = _dot(f_ref[...], chan_ref[...].astype(BF16))
            cols = slice(b * FNET_WIDTH, (b + 1) * FNET_WIDTH)
            yc_ref[:, cols] = y[:, :FNET_WIDTH].astype(BF16)
            ys_ref[:, cols] = y[:, FNET_WIDTH:].astype(BF16)

    out = (_dot(pos_ref[:, :DEC_SEQ].astype(BF16), yc_ref[...])
           - _dot(pos_ref[:, DEC_SEQ:].astype(BF16), ys_ref[...]))
    out = out * FNET_LAT_SCALE
    for b in range(DEC_BATCH):
        o_ref[b] = out[:, b * FNET_WIDTH:(b + 1) * FNET_WIDTH].astype(BF16)


def _fnet_lat(f, chan, pos):
    lat0 = CTX_TOKENS // DEC_SEQ
    return pl.pallas_call(
        _fnet_lat_kernel,
        out_shape=jax.ShapeDtypeStruct((DEC_BATCH, DEC_SEQ, FNET_WIDTH), BF16),
        grid=(DEC_SEQ // FNET_LAT_ROWS,),
        in_specs=[
            pl.BlockSpec((DEC_SEQ, FNET_WIDTH), lambda j: (lat0, 0)),
            pl.BlockSpec((DEC_SEQ, FNET_WIDTH), lambda j: (lat0 + 1, 0)),
            pl.BlockSpec((FNET_WIDTH, 2 * FNET_WIDTH), lambda j: (0, 0)),
            pl.BlockSpec((FNET_LAT_ROWS, 2 * DEC_SEQ), lambda j: (j, 0)),
        ],
        out_specs=pl.BlockSpec((DEC_BATCH, FNET_LAT_ROWS, FNET_WIDTH), lambda j: (0, j, 0)),
        scratch_shapes=[pltpu.VMEM((DEC_SEQ, DEC_BATCH * FNET_WIDTH), BF16)] * 2,
        compiler_params=_params(1),
        name="fnet_lat",
    )(f, f, chan, pos)


def _merge_kernel(h_ref, m_ref, g_ref, w_in_hbm, bg_ref, oac_ref, oal_ref, ob_ref, occ_ref, ocl_ref,
                  pa_hbm, pb_hbm, pc_hbm, wo_hbm, o_ref, wg_ref, pa_ref, pb_ref, pc_ref, wo_ref, stage_ref, sem_ref,
                  *, layer):
    @pl.when(pl.program_id(0) == 0)
    def _():
        jobs = (_row_chunks(w_in_hbm.at[layer, :, pl.ds(PROJ_COLS, GATE_COLS)], wg_ref, D_MODEL)
                + _row_chunks(pa_hbm.at[layer], pa_ref, ATTN_WIDTH)
                + _row_chunks(pb_hbm.at[layer], pb_ref, SGU_WIDTH)
                + _row_chunks(pc_hbm.at[layer], pc_ref, FNET_WIDTH)
                + _row_chunks(wo_hbm.at[layer], wo_ref, D_MODEL))
        _stage_weights(jobs, stage_ref, sem_ref)

    ctx = pl.program_id(0) < CTX_TILES
    x = h_ref[...]
    shift = m_ref[0, 3:4, :]
    scale = m_ref[0, 4:5, :]
    gate = m_ref[0, 5:6, :]
    z = _rms_modulate(x, g_ref[...], shift, scale).astype(BF16)
    oa = jnp.where(ctx, oac_ref[...], oal_ref[...])
    oc = jnp.where(ctx, occ_ref[...], ocl_ref[...])
    branches = ((oa, pa_ref), (ob_ref[...], pb_ref), (oc, pc_ref))
    mix = None
    for j, (o, p_ref) in enumerate(branches):
        cols = slice(j * D_MODEL, (j + 1) * D_MODEL)
        gj = _sigmoid(_dot(z, wg_ref[:, cols]) + bg_ref[j:j + 1, :])
        term = gj * _dot(o, p_ref[...])
        mix = term if mix is None else mix + term
    o_ref[...] = x + gate * _dot(mix.astype(BF16), wo_ref[...])


def _merge(h, mod, gains, w_in, b_gate, oa_ctx, oa_lat, ob, oc_ctx, oc_lat, p_attn, p_sgu, p_fnet, w_out, layer):
    lw = lambda i: (layer, 0, 0)
    ctx_tile = lambda i: (_ctx_tile(i), 0)
    lat_tile = lambda i: (_lat_tile(i), 0)
    hbm = pl.BlockSpec(memory_space=pl.ANY)
    return pl.pallas_call(
        functools.partial(_merge_kernel, layer=layer),
        out_shape=jax.ShapeDtypeStruct((TOKENS, D_MODEL), F32),
        grid=(N_TILES,),
        in_specs=[
            pl.BlockSpec((TM, D_MODEL), lambda i: (i, 0)),
            _mod_spec(layer),
            _gain_spec(layer, 1),
            hbm,
            pl.BlockSpec((None, N_BRANCH, D_MODEL), lw),
            pl.BlockSpec((TM, ATTN_WIDTH), ctx_tile),
            pl.BlockSpec((TM, ATTN_WIDTH), lat_tile),
            pl.BlockSpec((TM, SGU_WIDTH), lambda i: (i, 0)),
            pl.BlockSpec((TM, FNET_WIDTH), ctx_tile),
            pl.BlockSpec((TM, FNET_WIDTH), lat_tile),
            hbm, hbm, hbm, hbm,
        ],
        out_specs=pl.BlockSpec((TM, D_MODEL), lambda i: (i, 0)),
        scratch_shapes=[
            pltpu.VMEM((D_MODEL, GATE_COLS), BF16),
            pltpu.VMEM((ATTN_WIDTH, D_MODEL), BF16),
            pltpu.VMEM((SGU_WIDTH, D_MODEL), BF16),
            pltpu.VMEM((FNET_WIDTH, D_MODEL), BF16),
            pltpu.VMEM((D_MODEL, D_MODEL), BF16),
            pltpu.VMEM((STAGE_SLOTS, STAGE_ROWS, GATE_COLS), F32),
            pltpu.SemaphoreType.DMA((STAGE_SLOTS,)),
        ],
        compiler_params=_params(1),
        name="merge",
    )(h, mod, gains, w_in, b_gate, oa_ctx, oa_lat, ob, oc_ctx, oc_lat, p_attn, p_sgu, p_fnet, w_out)


def kernel(x_prompt, x_sample, cache_k, cache_v, c, c_ctx, w_mod, b_mod, g_norm, ffn_w_gate, ffn_w_up,
           ffn_w_down, w_in, b_gate, rpb, sgu_norm, sgu_w, sgu_b, p_attn, p_sgu, p_fnet, w_out, g_final):
    cond = jnp.concatenate([c_ctx[None, :], c, jnp.zeros((COND_ROWS - N_COND, D_MODEL), F32)], axis=0)
    mod = _modulation(cond, w_mod, b_mod).reshape(DEPTH, COND_ROWS, N_MOD, D_MODEL)
    rpb_pad = jnp.pad(rpb, ((0, 0), (0, 0), (0, 0), (0, RPB_LANES - N_DCOL)))
    cache_k = cache_k.reshape(DEC_BATCH, DEPTH, PAST_LEN, ATTN_WIDTH)
    cache_v = cache_v.reshape(DEC_BATCH, DEPTH, PAST_LEN, ATTN_WIDTH)
    gains = g_norm.reshape(DEPTH, 3, 1, D_MODEL)
    g_final2 = g_final.reshape(1, D_MODEL)

    sgu_gain = sgu_norm.reshape(DEPTH, 1, SGU_WIDTH)
    w_cat = jnp.transpose(sgu_w, (0, 2, 1, 3)).reshape(DEPTH, CHUNK, SGU_GROUPS * CHUNK)
    bias_tile = jnp.repeat(jnp.transpose(sgu_b, (0, 2, 1)), SGU_GROUP_DIM, axis=2)

    chan = jnp.asarray(_CHAN_TABLE)
    pos_ctx = jnp.asarray(_CTX_POS_TABLE)
    pos_lat = jnp.asarray(_LAT_POS_TABLE)

    kt = jnp.zeros((BATCH, DEPTH, ATTN_WIDTH, SEQ), F32)
    vt = jnp.zeros((BATCH, DEPTH, ATTN_WIDTH, SEQ), F32)

    xs = (x_prompt.reshape(CTX_TOKENS, D_MODEL), x_sample.reshape(LAT_TOKENS, D_MODEL))
    for l in range(DEPTH):
        h = _ffn(xs, mod, gains, ffn_w_gate, ffn_w_up, ffn_w_down, g_final2, l, 0, split_in=(l == 0))
        qkv, ob, f, oc_ctx, oa_ctx, kt, vt = _inproj(h, mod, gains, w_in, kt, vt, sgu_gain, w_cat, bias_tile, chan, pos_ctx, l)
        oa_lat = _natten(qkv, cache_k, cache_v, rpb_pad, l)
        oc_lat = _fnet_lat(f, chan, pos_lat).reshape(LAT_TOKENS, FNET_WIDTH)
        h = _merge(h, mod, gains, w_in, b_gate, oa_ctx, oa_lat, ob, oc_ctx, oc_lat, p_attn, p_sgu, p_fnet, w_out, l)
        xs = _ffn((h,), mod, gains, ffn_w_gate, ffn_w_up, ffn_w_down, g_final2, l, 2, split_out=(l == DEPTH - 1))
        xs = xs if l == DEPTH - 1 else (xs,)

    y_prompt, y_sample = xs
    to_cache = lambda t: jnp.transpose(t.reshape(BATCH, DEPTH, N_HEADS, HEAD_DIM, SEQ), (0, 1, 4, 2, 3))
    return (y_prompt.reshape(BATCH, SEQ, D_MODEL), y_sample.reshape(DEC_BATCH, DEC_SEQ, D_MODEL),
            to_cache(kt), to_cache(vt))
```

```python
import functools

import numpy as np
import jax
import jax.numpy as jnp
from jax import lax
from jax.experimental import pallas as pl
from jax.experimental.pallas import tpu as pltpu

D_MODEL = 1024
BATCH = 32
SEQ = 256
DEPTH = 4
DEC_BATCH = 2
DEC_SEQ = 2048
PAST_LEN = 256
GRID_W = 64
GRID_ROWS = DEC_SEQ // GRID_W
WIN_H = 8
WIN_W = 16
HEAD_DIM = 64
ATTN_WIDTH = 512
N_HEADS = 8
SGU_WIDTH = 256
SGU_GROUPS = 4
SGU_GROUP_DIM = 64
CHUNK = 128
FNET_WIDTH = 256
FNET_GROUPS = 4
FNET_GROUP_DIM = 64
N_BRANCH = 3
D_FF = 2816
N_MOD = 9
RMS_EPS = 1e-6
NEG_INF = -1e30

CTX_TOKENS = BATCH * SEQ
LAT_TOKENS = DEC_BATCH * DEC_SEQ
TOKENS = CTX_TOKENS + LAT_TOKENS
N_COND = 1 + DEC_BATCH
COND_ROWS = 8
PROJ_COLS = 3 * ATTN_WIDTH + 2 * SGU_WIDTH + FNET_WIDTH
GATE_COLS = N_BRANCH * D_MODEL

TM = 1024
SPLIT_TM = 512
N_TILES = TOKENS // TM
CTX_TILES = CTX_TOKENS // TM
REQ_PER_TILE = TM // SEQ
FF_CHUNK = 256
VMEM_LIMIT = 56 * 1024 * 1024

BF16 = jnp.bfloat16
F32 = jnp.float32
ATTN_SCALE = np.float32(HEAD_DIM ** -0.5)


def _cond_row(i, tm=TM):
    ctx_tiles = CTX_TOKENS // tm
    return jnp.where(i < ctx_tiles, 0, 1 + (i - ctx_tiles) // (DEC_SEQ // tm))


def _ctx_tile(i, tm=TM):
    return jnp.minimum(i, CTX_TOKENS // tm - 1)


def _lat_tile(i, tm=TM):
    return jnp.maximum(i - CTX_TOKENS // tm, 0)


def _params(n_axes):
    return pltpu.CompilerParams(dimension_semantics=("arbitrary",) * n_axes,
                                vmem_limit_bytes=VMEM_LIMIT)


def _dot(a, b):
    return jnp.dot(a, b, preferred_element_type=F32)


def _dot_nt(a, b):
    return lax.dot_general(a, b, (((1,), (1,)), ((), ())), preferred_element_type=F32)


def _rms_modulate(x, g, shift, scale):
    ms = jnp.mean(x * x, axis=-1, keepdims=True)
    y = x * lax.rsqrt(ms + RMS_EPS) * g
    return y * (1.0 + scale) + shift


def _silu(x):
    return x * (1.0 / (1.0 + jnp.exp(-x)))


def _sigmoid(x):
    return 1.0 / (1.0 + jnp.exp(-x))


def _gelu_tanh(x):
    c = np.float32(np.sqrt(2.0 / np.pi))
    return x * (0.5 * (1.0 + jnp.tanh(c * (x + np.float32(0.044715) * (x * x * x)))))


MOD_COLS = N_MOD * D_MODEL
MOD_BLOCK = 4608


def _mod_kernel(c_ref, w_ref, b_ref, o_ref):
    a = _silu(c_ref[...]).astype(BF16)
    o_ref[0] = _dot(a, w_ref[0].astype(BF16)) + b_ref[0]


def _modulation(cond, w_mod, b_mod):
    return pl.pallas_call(
        _mod_kernel,
        out_shape=jax.ShapeDtypeStruct((DEPTH, COND_ROWS, MOD_COLS), F32),
        grid=(DEPTH, MOD_COLS // MOD_BLOCK),
        in_specs=[
            pl.BlockSpec((COND_ROWS, D_MODEL), lambda l, j: (0, 0)),
            pl.BlockSpec((1, D_MODEL, MOD_BLOCK), lambda l, j: (l, 0, j)),
            pl.BlockSpec((1, 1, MOD_BLOCK), lambda l, j: (l, 0, j)),
        ],
        out_specs=pl.BlockSpec((1, COND_ROWS, MOD_BLOCK), lambda l, j: (l, 0, j)),
        compiler_params=_params(2),
        name="modulation",
    )(cond, w_mod, b_mod.reshape(DEPTH, 1, MOD_COLS))


def _mod_spec(layer, tm=TM):
    return pl.BlockSpec((None, 1, N_MOD, D_MODEL), lambda i: (layer, _cond_row(i, tm), 0, 0))


def _gain_spec(layer, sub):
    return pl.BlockSpec((None, None, 1, D_MODEL), lambda i: (layer, sub, 0, 0))


STAGE_ROWS = 256
STAGE_SLOTS = 2


def _stage_weights(jobs, stage_ref, sem_ref):
    def copy(k):
        src = jobs[k][0]
        rows, cols = src.shape
        return pltpu.make_async_copy(src, stage_ref.at[k % STAGE_SLOTS, :rows, :cols], sem_ref.at[k % STAGE_SLOTS])

    copy(0).start()
    for k, (src, dst) in enumerate(jobs):
        if k + 1 < len(jobs):
            copy(k + 1).start()
        copy(k).wait()
        rows, cols = src.shape
        dst[...] = stage_ref[k % STAGE_SLOTS, :rows, :cols].astype(BF16)


def _row_chunks(src, dst, rows):
    return [(src.at[pl.ds(r, STAGE_ROWS), :], dst.at[pl.ds(r, STAGE_ROWS), :]) for r in range(0, rows, STAGE_ROWS)]


N_FF_CHUNKS = D_FF // FF_CHUNK
FFN_WEIGHTS = 3


def _ffn_kernel(*refs, mod_base, split_in, split_out, layer, ffn_sub, tm):
    n_in = 2 if split_in else 1
    x_refs, (m_ref, g_ref, wg_hbm, wu_hbm, wd_hbm, gf_ref) = refs[:n_in], refs[n_in:n_in + 6]
    n_out = 2 if split_out else 1
    o_refs = refs[n_in + 6:n_in + 6 + n_out]
    z_ref, acc_ref, wg_ref, wu_ref, wd_ref, stage_ref, stage_d_ref, sem_ref = refs[n_in + 6 + n_out:]
    first_tile = pl.program_id(0) == 0
    is_ctx = pl.program_id(0) < CTX_TOKENS // tm

    def chunk(j):
        z = z_ref[...]
        act = (_silu(_dot(z, wg_ref[j])) * _dot(z, wu_ref[j])).astype(BF16)
        return _dot(act, wd_ref[j])

    def copies(j, slot):
        cols = pl.ds(pl.multiple_of(j * FF_CHUNK, FF_CHUNK), FF_CHUNK)
        srcs = (wg_hbm.at[layer, ffn_sub, :, cols], wu_hbm.at[layer, ffn_sub, :, cols], wd_hbm.at[layer, ffn_sub, cols, :])
        dsts = (stage_ref.at[0, slot], stage_ref.at[1, slot], stage_d_ref.at[slot])
        return [pltpu.make_async_copy(src, dst, sem_ref.at[w, slot]) for w, (src, dst) in enumerate(zip(srcs, dsts))]

    def staged_chunks():
        for j in range(STAGE_SLOTS):
            for c in copies(j, j):
                c.start()
        acc_ref[...] = jnp.zeros_like(acc_ref)

        def staged_chunk(j, carry):
            slot = j % STAGE_SLOTS
            for c in copies(j, slot):
                c.wait()
            wg_ref[j] = stage_ref[0, slot].astype(BF16)
            wu_ref[j] = stage_ref[1, slot].astype(BF16)
            wd_ref[j] = stage_d_ref[slot].astype(BF16)

            @pl.when(j + STAGE_SLOTS < N_FF_CHUNKS)
            def _():
                for c in copies(j + STAGE_SLOTS, slot):
                    c.start()

            acc_ref[...] += chunk(j)
            return carry

        lax.fori_loop(0, N_FF_CHUNKS, staged_chunk, 0)

    def unrolled_chunks():
        for j in range(N_FF_CHUNKS):
            if j == 0:
                acc_ref[...] = chunk(j)
            else:
                acc_ref[...] += chunk(j)

    def tile(run_chunks):
        x = jnp.where(is_ctx, x_refs[0][...], x_refs[1][...]) if split_in else x_refs[0][...]
        shift = m_ref[0, mod_base:mod_base + 1, :]
        scale = m_ref[0, mod_base + 1:mod_base + 2, :]
        gate = m_ref[0, mod_base + 2:mod_base + 3, :]
        z_ref[...] = _rms_modulate(x, g_ref[...], shift, scale).astype(BF16)
        run_chunks()
        y = x + (0.5 * gate) * acc_ref[...]
        if not split_out:
            o_refs[0][...] = y
        else:
            ms = jnp.mean(y * y, axis=-1, keepdims=True)
            y = y * lax.rsqrt(ms + RMS_EPS) * gf_ref[...]

            @pl.when(is_ctx)
            def _():
                o_refs[0][...] = y

            @pl.when(jnp.logical_not(is_ctx))
            def _():
                o_refs[1][...] = y

    @pl.when(first_tile)
    def _():
        tile(staged_chunks)

    @pl.when(jnp.logical_not(first_tile))
    def _():
        tile(unrolled_chunks)


def _ffn(xs, mod, gains, wg, wu, wd, g_final, layer, sub, *, split_in=False, split_out=False):
    tm = SPLIT_TM if (split_in or split_out) else TM
    kern = functools.partial(_ffn_kernel, mod_base=0 if sub == 0 else 6, split_in=split_in, split_out=split_out,
                             layer=layer, ffn_sub=0 if sub == 0 else 1, tm=tm)
    row = pl.BlockSpec((tm, D_MODEL), lambda i: (i, 0))
    split_rows = [pl.BlockSpec((tm, D_MODEL), lambda i: (_ctx_tile(i, tm), 0)),
                  pl.BlockSpec((tm, D_MODEL), lambda i: (_lat_tile(i, tm), 0))]
    hbm = pl.BlockSpec(memory_space=pl.ANY)
    if split_out:
        out_shape = (jax.ShapeDtypeStruct((CTX_TOKENS, D_MODEL), F32), jax.ShapeDtypeStruct((LAT_TOKENS, D_MODEL), F32))
        out_specs = tuple(split_rows)
    else:
        out_shape = jax.ShapeDtypeStruct((TOKENS, D_MODEL), F32)
        out_specs = row
    return pl.pallas_call(
        kern,
        out_shape=out_shape,
        grid=(TOKENS // tm,),
        in_specs=(split_rows if split_in else [row]) + [
            _mod_spec(layer, tm),
            _gain_spec(layer, sub),
            hbm, hbm, hbm,
            pl.BlockSpec((1, D_MODEL), lambda i: (0, 0)),
        ],
        out_specs=out_specs,
        scratch_shapes=[
            pltpu.VMEM((tm, D_MODEL), BF16),
            pltpu.VMEM((tm, D_MODEL), F32),
            pltpu.VMEM((N_FF_CHUNKS, D_MODEL, FF_CHUNK), BF16),
            pltpu.VMEM((N_FF_CHUNKS, D_MODEL, FF_CHUNK), BF16),
            pltpu.VMEM((N_FF_CHUNKS, FF_CHUNK, D_MODEL), BF16),
            pltpu.VMEM((2, STAGE_SLOTS, D_MODEL, FF_CHUNK), F32),
            pltpu.VMEM((STAGE_SLOTS, FF_CHUNK, D_MODEL), F32),
            pltpu.SemaphoreType.DMA((FFN_WEIGHTS, STAGE_SLOTS)),
        ],
        compiler_params=_params(1),
        name="ffn",
    )(*xs, mod, gains, wg, wu, wd, g_final)


def _sgu_tile(u, v, g_ref, w_ref, b_ref, o_ref):
    u = _gelu_tanh(u)
    v = _gelu_tanh(v)
    ms = jnp.mean(v * v, axis=-1, keepdims=True)
    vn = v * lax.rsqrt(ms + RMS_EPS) * g_ref[...]
    group = lax.broadcasted_iota(jnp.int32, (CHUNK, SGU_WIDTH), 1) // SGU_GROUP_DIM
    w = w_ref[...].astype(BF16)
    bias = b_ref[...]
    for n in range(TM // CHUNK):
        rows = slice(n * CHUNK, (n + 1) * CHUNK)
        x = vn[rows, :]
        stacked = jnp.concatenate(
            [jnp.where(group == g, x, 0.0) for g in range(SGU_GROUPS)], axis=0).astype(BF16)
        mixed = _dot(w, stacked) + bias
        o_ref[rows, :] = (u[rows, :] * mixed).astype(BF16)


def _fnet_ctx_tile(f, chan_ref, pos_ref, o_ref):
    y = _dot(f, chan_ref[...].astype(BF16))
    pos = pos_ref[...].astype(BF16)
    for b in range(REQ_PER_TILE):
        rows = slice(b * SEQ, (b + 1) * SEQ)
        yc = y[rows, :FNET_WIDTH].astype(BF16)
        ys = y[rows, FNET_WIDTH:].astype(BF16)
        out = _dot(pos[:, :SEQ], yc) - _dot(pos[:, SEQ:], ys)
        o_ref[rows, :] = (out * FNET_CTX_SCALE).astype(BF16)


def _inproj_kernel(h_ref, m_ref, g_ref, w_hbm, kt_in, vt_in, sg_ref, sw_ref, sb_ref, chan_ref, pos_ref,
                   pa_ref, ob_ref, pf_ref, occ_ref, oac_ref, kt_ref, vt_ref,
                   w_ref, stage_ref, sem_ref, z_ref, kf_ref, vf_ref, *, layer):
    del kt_in, vt_in

    @pl.when(pl.program_id(0) == 0)
    def _():
        _stage_weights(_row_chunks(w_hbm.at[layer, :, pl.ds(0, PROJ_COLS)], w_ref, D_MODEL), stage_ref, sem_ref)

    is_ctx = pl.program_id(0) < CTX_TILES
    q_cols, k_cols, v_cols = (slice(c * ATTN_WIDTH, (c + 1) * ATTN_WIDTH) for c in range(3))
    rest_cols = slice(3 * ATTN_WIDTH, PROJ_COLS)

    def project(cols):
        return _dot(z_ref[...], w_ref[:, cols])

    def tile(write_cache):
        shift = m_ref[0, 3:4, :]
        scale = m_ref[0, 4:5, :]
        z_ref[...] = _rms_modulate(h_ref[...], g_ref[...], shift, scale).astype(BF16)
        for cols, full_ref, cache_ref in ((k_cols, kf_ref, kt_ref), (v_cols, vf_ref, vt_ref)):
            y = project(cols)
            pa_ref[:, cols] = y.astype(BF16)
            if write_cache:
                full_ref[...] = y
                for b in range(REQ_PER_TILE):
                    cache_ref[b] = full_ref[b * SEQ:(b + 1) * SEQ, :].T
        pa_ref[:, q_cols] = (project(q_cols) * ATTN_SCALE).astype(BF16)
        if write_cache:
            _ctx_attention_tile(pa_ref, kt_ref, oac_ref)
        rest = project(rest_cols)
        _sgu_tile(rest[:, :SGU_WIDTH], rest[:, SGU_WIDTH:2 * SGU_WIDTH], sg_ref, sw_ref, sb_ref, ob_ref)
        f = rest[:, 2 * SGU_WIDTH:].astype(BF16)
        pf_ref[...] = f
        if write_cache:
            _fnet_ctx_tile(f, chan_ref, pos_ref, occ_ref)

    @pl.when(is_ctx)
    def _():
        tile(True)

    @pl.when(jnp.logical_not(is_ctx))
    def _():
        tile(False)


def _inproj(h, mod, gains, w_in, kt, vt, sgu_norm, w_cat, bias_tile, chan, pos_ctx, layer):
    cache_shape = jax.ShapeDtypeStruct((BATCH, DEPTH, ATTN_WIDTH, SEQ), F32)
    cache_spec = pl.BlockSpec((REQ_PER_TILE, None, ATTN_WIDTH, SEQ), lambda i: (_ctx_tile(i), layer, 0, 0))
    row = lambda width: pl.BlockSpec((TM, width), lambda i: (i, 0))
    return pl.pallas_call(
        functools.partial(_inproj_kernel, layer=layer),
        out_shape=(jax.ShapeDtypeStruct((TOKENS, 3 * ATTN_WIDTH), BF16),
                   jax.ShapeDtypeStruct((TOKENS, SGU_WIDTH), BF16),
                   jax.ShapeDtypeStruct((TOKENS, FNET_WIDTH), BF16),
                   jax.ShapeDtypeStruct((CTX_TOKENS, FNET_WIDTH), BF16),
                   jax.ShapeDtypeStruct((CTX_TOKENS, ATTN_WIDTH), BF16), cache_shape, cache_shape),
        grid=(N_TILES,),
        in_specs=[
            row(D_MODEL),
            _mod_spec(layer),
            _gain_spec(layer, 1),
            pl.BlockSpec(memory_space=pl.ANY),
            pl.BlockSpec(memory_space=pl.ANY),
            pl.BlockSpec(memory_space=pl.ANY),
            pl.BlockSpec((None, 1, SGU_WIDTH), lambda i: (layer, 0, 0)),
            pl.BlockSpec((None, CHUNK, SGU_GROUPS * CHUNK), lambda i: (layer, 0, 0)),
            pl.BlockSpec((None, CHUNK, SGU_WIDTH), lambda i: (layer, 0, 0)),
            pl.BlockSpec((FNET_WIDTH, 2 * FNET_WIDTH), lambda i: (0, 0)),
            pl.BlockSpec((SEQ, 2 * SEQ), lambda i: (0, 0)),
        ],
        out_specs=(row(3 * ATTN_WIDTH), row(SGU_WIDTH), row(FNET_WIDTH),
                   pl.BlockSpec((TM, FNET_WIDTH), lambda i: (_ctx_tile(i), 0)),
                   pl.BlockSpec((TM, ATTN_WIDTH), lambda i: (_ctx_tile(i), 0)), cache_spec, cache_spec),
        input_output_aliases={4: 5, 5: 6},
        scratch_shapes=[
            pltpu.VMEM((D_MODEL, PROJ_COLS), BF16),
            pltpu.VMEM((STAGE_SLOTS, STAGE_ROWS, PROJ_COLS), F32),
            pltpu.SemaphoreType.DMA((STAGE_SLOTS,)),
            pltpu.VMEM((TM, D_MODEL), BF16),
            pltpu.VMEM((TM, ATTN_WIDTH), F32),
            pltpu.VMEM((TM, ATTN_WIDTH), F32),
        ],
        compiler_params=_params(1),
        name="inproj",
    )(h, mod, gains, w_in, kt, vt, sgu_norm, w_cat, bias_tile, chan, pos_ctx)


HEAD_PAIRS = N_HEADS // 2
PAIR_WIDTH = 2 * HEAD_DIM
PAIR_COLS = [slice(j * PAIR_WIDTH, (j + 1) * PAIR_WIDTH) for j in range(HEAD_PAIRS)]


def _first_of_pair(rows):
    return lax.broadcasted_iota(jnp.int32, (rows, PAIR_WIDTH), 1) < HEAD_DIM


def _pair_queries(q):
    first = _first_of_pair(q.shape[0])
    zero = jnp.zeros_like(q)
    return jnp.concatenate([jnp.where(first, q, zero), jnp.where(first, zero, q)], axis=0)


def _pair_outputs(o):
    m = o.shape[0] // 2
    return jnp.where(_first_of_pair(m), o[:m], o[m:])


def _ctx_attention_tile(qkv_ref, kt_ref, o_ref):
    for b in range(REQ_PER_TILE):
        rows = slice(b * SEQ, (b + 1) * SEQ)
        q = qkv_ref[rows, :ATTN_WIDTH]
        kt = kt_ref[b].astype(BF16)
        v = qkv_ref[rows, 2 * ATTN_WIDTH:]
        s = jnp.concatenate([_dot(_pair_queries(q[:, c]), kt[c, :]) for c in PAIR_COLS], axis=0)
        e = jnp.exp(s - jnp.max(s, axis=-1, keepdims=True))
        inv = 1.0 / jnp.sum(e, axis=-1, keepdims=True)
        p = e.astype(BF16)
        outs = []
        for j, c in enumerate(PAIR_COLS):
            pair = slice(2 * j * SEQ, 2 * (j + 1) * SEQ)
            outs.append(_pair_outputs(_dot(p[pair, :], v[:, c]) * inv[pair, :]))
        o_ref[rows, :] = jnp.concatenate(outs, axis=-1).astype(BF16)


N_DROW = 2 * WIN_H - 1
N_DCOL = 2 * WIN_W - 1
N_DROW_PAIRS = N_DROW - 1
KEY_ROWS = WIN_H
KEYS_LOCAL = KEY_ROWS * GRID_W
RPB_LANES = 2 * GRID_W


def _build_bias_table(rpb_ref, tab_ref):
    first = _first_of_pair(GRID_W)
    for h in range(N_HEADS):
        for d in range(N_DROW_PAIRS):
            lo = jnp.broadcast_to(rpb_ref[h, d:d + 1, :], (GRID_W, RPB_LANES))
            hi = jnp.broadcast_to(rpb_ref[h, d + 1:d + 2, :], (GRID_W, RPB_LANES))
            lo = pltpu.roll(lo, RPB_LANES - (WIN_W - 1), 1, stride=1, stride_axis=0)
            hi = pltpu.roll(hi, GRID_W - (WIN_W - 1), 1, stride=1, stride_axis=0)
            tab_ref[h, d] = jnp.where(first, lo, hi)


def _natten_kernel(qb_ref, kb_ref, vb_ref, kc_ref, vc_ref, rpb_ref, o_ref, tab_ref):
    @pl.when(pl.program_id(0) == 0)
    def _():
        _build_bias_table(rpb_ref, tab_ref)

    kc = kc_ref[...].astype(BF16)
    vc = vc_ref[...].astype(BF16)
    qcol = lax.broadcasted_iota(jnp.int32, (N_HEADS * GRID_W, KEYS_LOCAL), 0) % GRID_W
    kcol = lax.broadcasted_iota(jnp.int32, (N_HEADS * GRID_W, KEYS_LOCAL), 1) % GRID_W
    start = jnp.clip(qcol - WIN_W // 2, 0, GRID_W - WIN_W)
    col_ok = (kcol >= start) & (kcol < start + WIN_W)

    def one_row(r, carry):
        r0 = jnp.clip(r - KEY_ROWS // 2, 0, GRID_ROWS - KEY_ROWS)
        d0 = r0 - r + (WIN_H - 1)
        qrows = pl.ds(pl.multiple_of(r * GRID_W, GRID_W), GRID_W)
        krows = pl.ds(pl.multiple_of(r0 * GRID_W, GRID_W), KEYS_LOCAL)
        q = qb_ref[qrows, :]
        k = jnp.concatenate([kb_ref[krows, :], kc], axis=0)
        v = jnp.concatenate([vb_ref[krows, :], vc], axis=0)
        s = jnp.concatenate([_dot_nt(_pair_queries(q[:, c]), k[:, c]) for c in PAIR_COLS], axis=0)
        bias = jnp.concatenate(
            [jnp.concatenate([tab_ref[h, d0 + 2 * i] for i in range(KEY_ROWS // 2)], axis=-1)
             for h in range(N_HEADS)], axis=0)
        s_loc = jnp.where(col_ok, s[:, :KEYS_LOCAL] + bias, NEG_INF)
        s_ctx = s[:, KEYS_LOCAL:]
        mx = jnp.maximum(jnp.max(s_loc, axis=-1, keepdims=True), jnp.max(s_ctx, axis=-1, keepdims=True))
        e_loc = jnp.exp(s_loc - mx)
        e_ctx = jnp.exp(s_ctx - mx)
        inv = 1.0 / (jnp.sum(e_loc, axis=-1, keepdims=True) + jnp.sum(e_ctx, axis=-1, keepdims=True))
        p = jnp.concatenate([e_loc, e_ctx], axis=-1).astype(BF16)
        outs = []
        for j, c in enumerate(PAIR_COLS):
            pair = slice(2 * j * GRID_W, 2 * (j + 1) * GRID_W)
            outs.append(_pair_outputs(_dot(p[pair, :], v[:, c]) * inv[pair, :]))
        o_ref[qrows, :] = jnp.concatenate(outs, axis=-1).astype(BF16)
        return carry

    lax.fori_loop(0, GRID_ROWS, one_row, 0, unroll=8)


def _natten(qkv, cache_k, cache_v, rpb_pad, layer):
    lat0 = CTX_TOKENS // DEC_SEQ
    spec = lambda c: pl.BlockSpec((DEC_SEQ, ATTN_WIDTH), lambda b, c=c: (lat0 + b, c))
    cache_spec = pl.BlockSpec((None, None, PAST_LEN, ATTN_WIDTH), lambda b: (b, layer, 0, 0))
    return pl.pallas_call(
        _natten_kernel,
        out_shape=jax.ShapeDtypeStruct((LAT_TOKENS, ATTN_WIDTH), BF16),
        grid=(DEC_BATCH,),
        in_specs=[spec(0), spec(1), spec(2), cache_spec, cache_spec,
                  pl.BlockSpec((None, N_HEADS, N_DROW, RPB_LANES), lambda b: (layer, 0, 0, 0))],
        out_specs=pl.BlockSpec((DEC_SEQ, ATTN_WIDTH), lambda b: (b, 0)),
        scratch_shapes=[pltpu.VMEM((N_HEADS, N_DROW_PAIRS, GRID_W, RPB_LANES), F32)],
        compiler_params=_params(1),
        name="natten",
    )(qkv, qkv, qkv, cache_k, cache_v, rpb_pad)


def _dft_tables(n):
    k = np.arange(n, dtype=np.int64)
    ang = 2.0 * np.pi * ((k[:, None] * k[None, :]) % n).astype(np.float64) / n
    return np.cos(ang), np.sin(ang)


def _channel_tables():
    c, s = _dft_tables(FNET_GROUP_DIM)
    eye = np.eye(FNET_GROUPS)
    return np.concatenate([np.kron(eye, c), np.kron(eye, s)], axis=1)


_CHAN_TABLE = _channel_tables().astype(np.float32)
_CTX_POS_TABLE = np.concatenate(_dft_tables(SEQ), axis=1).astype(np.float32)
_LAT_POS_TABLE = np.concatenate(_dft_tables(DEC_SEQ), axis=1).astype(np.float32)
FNET_CTX_SCALE = np.float32(1.0 / np.sqrt(SEQ * FNET_GROUP_DIM))
FNET_LAT_SCALE = np.float32(1.0 / np.sqrt(DEC_SEQ * FNET_GROUP_DIM))


FNET_LAT_ROWS = 512


def _fnet_lat_kernel(f0_ref, f1_ref, chan_ref, pos_ref, o_ref, yc_ref, ys_ref):
    @pl.when(pl.program_id(0) == 0)
    def _():
        for b, f_ref in enumerate((f0_ref, f1_ref)):
            y = _dot(f_ref[...], chan_ref[...].astype(BF16))
            cols = slice(b * FNET_WIDTH, (b + 1) * FNET_WIDTH)
            yc_ref[:, cols] = y[:, :FNET_WIDTH].astype(BF16)
            ys_ref[:, cols] = y[:, FNET_WIDTH:].astype(BF16)

    out = (_dot(pos_ref[:, :DEC_SEQ].astype(BF16), yc_ref[...])
           - _dot(pos_ref[:, DEC_SEQ:].astype(BF16), ys_ref[...]))
    out = out * FNET_LAT_SCALE
    for b in range(DEC_BATCH):
        o_ref[b] = out[:, b * FNET_WIDTH:(b + 1) * FNET_WIDTH].astype(BF16)


def _fnet_lat(f, chan, pos):
    lat0 = CTX_TOKENS // DEC_SEQ
    return pl.pallas_call(
        _fnet_lat_kernel,
        out_shape=jax.ShapeDtypeStruct((DEC_BATCH, DEC_SEQ, FNET_WIDTH), BF16),
        grid=(DEC_SEQ // FNET_LAT_ROWS,),
        in_specs=[
            pl.BlockSpec((DEC_SEQ, FNET_WIDTH), lambda j: (lat0, 0)),
            pl.BlockSpec((DEC_SEQ, FNET_WIDTH), lambda j: (lat0 + 1, 0)),
            pl.BlockSpec((FNET_WIDTH, 2 * FNET_WIDTH), lambda j: (0, 0)),
            pl.BlockSpec((FNET_LAT_ROWS, 2 * DEC_SEQ), lambda j: (j, 0)),
        ],
        out_specs=pl.BlockSpec((DEC_BATCH, FNET_LAT_ROWS, FNET_WIDTH), lambda j: (0, j, 0)),
        scratch_shapes=[pltpu.VMEM((DEC_SEQ, DEC_BATCH * FNET_WIDTH), BF16)] * 2,
        compiler_params=_params(1),
        name="fnet_lat",
    )(f, f, chan, pos)


def _merge_kernel(h_ref, m_ref, g_ref, w_in_hbm, bg_ref, oac_ref, oal_ref, ob_ref, occ_ref, ocl_ref,
                  pa_hbm, pb_hbm, pc_hbm, wo_hbm, o_ref, wg_ref, pa_ref, pb_ref, pc_ref, wo_ref, stage_ref, sem_ref,
                  *, layer):
    @pl.when(pl.program_id(0) == 0)
    def _():
        jobs = (_row_chunks(w_in_hbm.at[layer, :, pl.ds(PROJ_COLS, GATE_COLS)], wg_ref, D_MODEL)
                + _row_chunks(pa_hbm.at[layer], pa_ref, ATTN_WIDTH)
                + _row_chunks(pb_hbm.at[layer], pb_ref, SGU_WIDTH)
                + _row_chunks(pc_hbm.at[layer], pc_ref, FNET_WIDTH)
                + _row_chunks(wo_hbm.at[layer], wo_ref, D_MODEL))
        _stage_weights(jobs, stage_ref, sem_ref)

    ctx = pl.program_id(0) < CTX_TILES
    x = h_ref[...]
    shift = m_ref[0, 3:4, :]
    scale = m_ref[0, 4:5, :]
    gate = m_ref[0, 5:6, :]
    z = _rms_modulate(x, g_ref[...], shift, scale).astype(BF16)
    oa = jnp.where(ctx, oac_ref[...], oal_ref[...])
    oc = jnp.where(ctx, occ_ref[...], ocl_ref[...])
    branches = ((oa, pa_ref), (ob_ref[...], pb_ref), (oc, pc_ref))
    mix = None
    for j, (o, p_ref) in enumerate(branches):
        cols = slice(j * D_MODEL, (j + 1) * D_MODEL)
        gj = _sigmoid(_dot(z, wg_ref[:, cols]) + bg_ref[j:j + 1, :])
        term = gj * _dot(o, p_ref[...])
        mix = term if mix is None else mix + term
    o_ref[...] = x + gate * _dot(mix.astype(BF16), wo_ref[...])


def _merge(h, mod, gains, w_in, b_gate, oa_ctx, oa_lat, ob, oc_ctx, oc_lat, p_attn, p_sgu, p_fnet, w_out, layer):
    lw = lambda i: (layer, 0, 0)
    ctx_tile = lambda i: (_ctx_tile(i), 0)
    lat_tile = lambda i: (_lat_tile(i), 0)
    hbm = pl.BlockSpec(memory_space=pl.ANY)
    return pl.pallas_call(
        functools.partial(_merge_kernel, layer=layer),
        out_shape=jax.ShapeDtypeStruct((TOKENS, D_MODEL), F32),
        grid=(N_TILES,),
        in_specs=[
            pl.BlockSpec((TM, D_MODEL), lambda i: (i, 0)),
            _mod_spec(layer),
            _gain_spec(layer, 1),
            hbm,
            pl.BlockSpec((None, N_BRANCH, D_MODEL), lw),
            pl.BlockSpec((TM, ATTN_WIDTH), ctx_tile),
            pl.BlockSpec((TM, ATTN_WIDTH), lat_tile),
            pl.BlockSpec((TM, SGU_WIDTH), lambda i: (i, 0)),
            pl.BlockSpec((TM, FNET_WIDTH), ctx_tile),
            pl.BlockSpec((TM, FNET_WIDTH), lat_tile),
            hbm, hbm, hbm, hbm,
        ],
        out_specs=pl.BlockSpec((TM, D_MODEL), lambda i: (i, 0)),
        scratch_shapes=[
            pltpu.VMEM((D_MODEL, GATE_COLS), BF16),
            pltpu.VMEM((ATTN_WIDTH, D_MODEL), BF16),
            pltpu.VMEM((SGU_WIDTH, D_MODEL), BF16),
            pltpu.VMEM((FNET_WIDTH, D_MODEL), BF16),
            pltpu.VMEM((D_MODEL, D_MODEL), BF16),
            pltpu.VMEM((STAGE_SLOTS, STAGE_ROWS, GATE_COLS), F32),
            pltpu.SemaphoreType.DMA((STAGE_SLOTS,)),
        ],
        compiler_params=_params(1),
        name="merge",
    )(h, mod, gains, w_in, b_gate, oa_ctx, oa_lat, ob, oc_ctx, oc_lat, p_attn, p_sgu, p_fnet, w_out)


def kernel(x_prompt, x_sample, cache_k, cache_v, c, c_ctx, w_mod, b_mod, g_norm, ffn_w_gate, ffn_w_up,
           ffn_w_down, w_in, b_gate, rpb, sgu_norm, sgu_w, sgu_b, p_attn, p_sgu, p_fnet, w_out, g_final):
    cond = jnp.concatenate([c_ctx[None, :], c, jnp.zeros((COND_ROWS - N_COND, D_MODEL), F32)], axis=0)
    mod = _modulation(cond, w_mod, b_mod).reshape(DEPTH, COND_ROWS, N_MOD, D_MODEL)
    rpb_pad = jnp.pad(rpb, ((0, 0), (0, 0), (0, 0), (0, RPB_LANES - N_DCOL)))
    cache_k = cache_k.reshape(DEC_BATCH, DEPTH, PAST_LEN, ATTN_WIDTH)
    cache_v = cache_v.reshape(DEC_BATCH, DEPTH, PAST_LEN, ATTN_WIDTH)
    gains = g_norm.reshape(DEPTH, 3, 1, D_MODEL)
    g_final2 = g_final.reshape(1, D_MODEL)

    sgu_gain = sgu_norm.reshape(DEPTH, 1, SGU_WIDTH)
    w_cat = jnp.transpose(sgu_w, (0, 2, 1, 3)).reshape(DEPTH, CHUNK, SGU_GROUPS * CHUNK)
    bias_tile = jnp.repeat(jnp.transpose(sgu_b, (0, 2, 1)), SGU_GROUP_DIM, axis=2)

    chan = jnp.asarray(_CHAN_TABLE)
    pos_ctx = jnp.asarray(_CTX_POS_TABLE)
    pos_lat = jnp.asarray(_LAT_POS_TABLE)

    kt = jnp.zeros((BATCH, DEPTH, ATTN_WIDTH, SEQ), F32)
    vt = jnp.zeros((BATCH, DEPTH, ATTN_WIDTH, SEQ), F32)

    xs = (x_prompt.reshape(CTX_TOKENS, D_MODEL), x_sample.reshape(LAT_TOKENS, D_MODEL))
    for l in range(DEPTH):
        h = _ffn(xs, mod, gains, ffn_w_gate, ffn_w_up, ffn_w_down, g_final2, l, 0, split_in=(l == 0))
        qkv, ob, f, oc_ctx, oa_ctx, kt, vt = _inproj(h, mod, gains, w_in, kt, vt, sgu_gain, w_cat, bias_tile, chan, pos_ctx, l)
        oa_lat = _natten(qkv, cache_k, cache_v, rpb_pad, l)
        oc_lat = _fnet_lat(f, chan, pos_lat).reshape(LAT_TOKENS, FNET_WIDTH)
        h = _merge(h, mod, gains, w_in, b_gate, oa_ctx, oa_lat, ob, oc_ctx, oc_lat, p_attn, p_sgu, p_fnet, w_out, l)
        xs = _ffn((h,), mod, gains, ffn_w_gate, ffn_w_up, ffn_w_down, g_final2, l, 2, split_out=(l == DEPTH - 1))
        xs = xs if l == DEPTH - 1 else (xs,)

    y_prompt, y_sample = xs
    to_cache = lambda t: jnp.transpose(t.reshape(BATCH, DEPTH, N_HEADS, HEAD_DIM, SEQ), (0, 1, 4, 2, 3))
    return (y_prompt.reshape(BATCH, SEQ, D_MODEL), y_sample.reshape(DEC_BATCH, DEC_SEQ, D_MODEL),
            to_cache(kt), to_cache(vt))
```
